```python
import functools
import jax, jax.numpy as jnp
from jax import lax
import numpy as np

D_MODEL = 4096
BATCH = 4
SEQ = 2048
DEPTH = 4
DEC_BATCH = 8
DEC_SEQ = 1
PAST_LEN = 8192
PAGE_SIZE = 128

CONV_WIDTH = D_MODEL // 4
CONV_K = 3
FOX_HEAD_DIM = 128
FOX_HEADS = (D_MODEL // 2) // FOX_HEAD_DIM
FOX_WIDTH = FOX_HEADS * FOX_HEAD_DIM
FOX_SCALE = FOX_HEAD_DIM ** -0.5
FORGET_BIAS = 4.0
Q_BLOCK = 128
SSM_GROUP = 16
SSM_WIDTH = D_MODEL // 4
SSM_GROUPS = SSM_WIDTH // SSM_GROUP
SSM_STATE = 64
DT_MIN = 0.001
DT_MAX = 0.1
MIX_WIDTH = CONV_WIDTH + FOX_WIDTH + SSM_WIDTH
D_FF = -(-8 * D_MODEL // (3 * 256)) * 256
RMS_EPS = 1e-6
IN_SIZES = (CONV_WIDTH,) * 3 + (FOX_WIDTH,) * 3 + (FOX_HEADS, SSM_WIDTH) + (D_MODEL,) * 3
IN_OFFSETS = tuple(int(o) for o in np.cumsum(IN_SIZES)[:-1])
N_IN = int(sum(IN_SIZES))

kernel_name = "hybrid_gatedconv_fox_s5_decoder_step"


def rmsnorm(x, g):
    xf = x.astype(jnp.float32)
    y = xf * lax.rsqrt(jnp.mean(xf * xf, axis=-1, keepdims=True) + RMS_EPS)
    return (y * g.astype(jnp.float32)).astype(x.dtype)


def short_conv(u, buf, w):
    T = u.shape[1]
    xp = jnp.concatenate([buf.astype(u.dtype), u], axis=1)
    y = sum(xp[:, j:j + T, :] * w[:, j] for j in range(CONV_K))
    return y, xp[:, T:, :]


def fox_prompt(q, k, v, logf):
    Bn, T, H, Dh = q.shape
    F = jnp.cumsum(logf, axis=1).transpose(0, 2, 1)
    kpos = jnp.arange(T)

    def one_block(i):
        start = i * Q_BLOCK
        qb = lax.dynamic_slice_in_dim(q, start, Q_BLOCK, axis=1)
        Fq = lax.dynamic_slice_in_dim(F, start, Q_BLOCK, axis=2)
        s = jnp.einsum('bqhd,bkhd->bhqk', qb, k).astype(jnp.float32) * FOX_SCALE
        logits = s + Fq[..., None] - F[:, :, None, :]
        qpos = start + jnp.arange(Q_BLOCK)
        mask = kpos[None, :] <= qpos[:, None]
        p = jax.nn.softmax(jnp.where(mask, logits, -jnp.inf), axis=-1)
        return jnp.einsum('bhqk,bkhd->bqhd', p.astype(v.dtype), v)

    out = lax.map(one_block, jnp.arange(T // Q_BLOCK))
    return out.transpose(1, 0, 2, 3, 4).reshape(Bn, T, H, Dh)


def fox_sample(q, k, v, logf, k_past, v_past, logf_past):
    T = q.shape[1]
    P = k_past.shape[1]
    lp = logf_past.astype(jnp.float32)
    after = lax.cumsum(lp, axis=1, reverse=True) - lp
    Fn = jnp.cumsum(logf, axis=1).transpose(0, 2, 1)
    s_past = (jnp.einsum('bqhd,bkhd->bhqk', q, k_past).astype(jnp.float32) * FOX_SCALE
              + Fn[..., None] + after.transpose(0, 2, 1)[:, :, None, :])
    s_new = (jnp.einsum('bqhd,bkhd->bhqk', q, k).astype(jnp.float32) * FOX_SCALE
             + Fn[..., None] - Fn[:, :, None, :])
    causal = jnp.tril(jnp.ones((T, T), dtype=bool))
    s_new = jnp.where(causal, s_new, -jnp.inf)
    p = jax.nn.softmax(jnp.concatenate([s_past, s_new], axis=-1), axis=-1).astype(v.dtype)
    return (jnp.einsum('bhqk,bkhd->bqhd', p[..., :P], v_past)
            + jnp.einsum('bhqk,bkhd->bqhd', p[..., P:], v))


def _complex_affine_combine(e1, e2):
    a1r, a1i, b1r, b1i = e1
    a2r, a2i, b2r, b2i = e2
    return (a2r * a1r - a2i * a1i,
            a2r * a1i + a2i * a1r,
            a2r * b1r - a2i * b1i + b2r,
            a2r * b1i + a2i * b1r + b2i)


def s5_mixer(u, h0_re, h0_im, a_re, a_im, log_dt, b_re, b_im, c_re, c_im, d_skip, w_glu):
    f32 = jnp.float32
    Bn, T, _ = u.shape
    ug = u.astype(f32).reshape(Bn, T, SSM_GROUPS, SSM_GROUP)
    a_re = a_re.astype(f32)
    a_im = a_im.astype(f32)
    dt = jnp.exp(log_dt.astype(f32))[:, None]
    mag = jnp.exp(a_re * dt)
    ang = a_im * dt
    abar_re = mag * jnp.cos(ang)
    abar_im = mag * jnp.sin(ang)
    den = a_re * a_re + a_im * a_im
    num_re = abar_re - 1.0
    zoh_re = (num_re * a_re + abar_im * a_im) / den
    zoh_im = (abar_im * a_re - num_re * a_im) / den
    br = b_re.astype(f32)
    bi = b_im.astype(f32)
    bbar_re = zoh_re[..., None] * br - zoh_im[..., None] * bi
    bbar_im = zoh_re[..., None] * bi + zoh_im[..., None] * br
    bu_re = jnp.einsum('btgc,gpc->btgp', ug, bbar_re)
    bu_im = jnp.einsum('btgc,gpc->btgp', ug, bbar_im)
    shp = bu_re.shape
    acum_re, acum_im, hs_re, hs_im = lax.associative_scan(
        _complex_affine_combine,
        (jnp.broadcast_to(abar_re, shp), jnp.broadcast_to(abar_im, shp), bu_re, bu_im),
        axis=1)
    h0r = h0_re.astype(f32)[:, None]
    h0i = h0_im.astype(f32)[:, None]
    h_re = hs_re + acum_re * h0r - acum_im * h0i
    h_im = hs_im + acum_re * h0i + acum_im * h0r
    y = (jnp.einsum('btgp,gcp->btgc', h_re, c_re.astype(f32))
         - jnp.einsum('btgp,gcp->btgc', h_im, c_im.astype(f32)))
    y = y + d_skip.astype(f32).reshape(SSM_GROUPS, SSM_GROUP) * ug
    z = jax.nn.gelu(y.reshape(Bn, T, SSM_WIDTH)).astype(u.dtype)
    out = z * jax.nn.sigmoid(z @ w_glu)
    return out, h_re[:, -1].astype(u.dtype), h_im[:, -1].astype(u.dtype)


def trunk_layer(x, conv_buf, h_re, h_im, attend, norm_mix, w_in, conv_w, fox_b,
                a_re, a_im, log_dt, b_re, b_im, c_re, c_im, d_skip, w_glu,
                w_branch, w_out, norm_ffn, w_up, w_down):
    Bn, T, _ = x.shape
    xn = rmsnorm(x, norm_mix)
    proj = xn @ w_in
    (h_a, b_a, c_a, q, k, v, f_logit, u_c, g_a, g_b, g_c) = jnp.split(proj, IN_OFFSETS, axis=-1)
    conv_out, conv_new = short_conv(c_a * h_a, conv_buf, conv_w)
    y_a = b_a * conv_out
    q = q.reshape(Bn, T, FOX_HEADS, FOX_HEAD_DIM)
    k = k.reshape(Bn, T, FOX_HEADS, FOX_HEAD_DIM)
    v = v.reshape(Bn, T, FOX_HEADS, FOX_HEAD_DIM)
    logf = jax.nn.log_sigmoid(f_logit.astype(jnp.float32) + fox_b.astype(jnp.float32))
    y_b = attend(q, k, v, logf).reshape(Bn, T, FOX_WIDTH)
    y_c, h_re_new, h_im_new = s5_mixer(u_c, h_re, h_im, a_re, a_im, log_dt, b_re, b_im,
                                       c_re, c_im, d_skip, w_glu)
    wa, wb, wc = jnp.split(w_branch, [CONV_WIDTH, CONV_WIDTH + FOX_WIDTH], axis=0)
    merged = (jax.nn.sigmoid(g_a) * (y_a @ wa) + jax.nn.sigmoid(g_b) * (y_b @ wb)
              + jax.nn.sigmoid(g_c) * (y_c @ wc))
    x = x + merged @ w_out
    gate, up = jnp.split(rmsnorm(x, norm_ffn) @ w_up, 2, axis=-1)
    x = x + (jax.nn.silu(gate) * up) @ w_down
    return x, k, v, logf.astype(x.dtype), conv_new, h_re_new, h_im_new


def setup_inputs(seed: int = 0) -> dict:
    key = jax.random.key(seed)
    ks = iter(jax.random.split(key, 40))
    f32 = jnp.float32

    def nrm(shape, scale):
        return jax.random.normal(next(ks), shape, f32) * scale

    n_pages = PAST_LEN // PAGE_SIZE
    n_used = DEC_BATCH * n_pages
    n_phys = n_used + max(1, n_used // 4)
    page_table = jax.random.permutation(next(ks), n_phys)[:n_used].reshape(DEC_BATCH, n_pages).astype(jnp.int32)
    log_dt = jnp.log(DT_MIN) + jax.random.uniform(next(ks), (DEPTH, SSM_GROUPS), f32) * (jnp.log(DT_MAX) - jnp.log(DT_MIN))
    w_branch = jnp.concatenate([nrm((DEPTH, CONV_WIDTH, D_MODEL), CONV_WIDTH ** -0.5),
                                nrm((DEPTH, FOX_WIDTH, D_MODEL), FOX_WIDTH ** -0.5),
                                nrm((DEPTH, SSM_WIDTH, D_MODEL), SSM_WIDTH ** -0.5)], axis=1)
    return {
        "x_prompt": nrm((BATCH, SEQ, D_MODEL), 1.0),
        "x_sample": nrm((DEC_BATCH, DEC_SEQ, D_MODEL), 1.0),
        "cache_k": nrm((DEPTH, n_phys, PAGE_SIZE, FOX_HEADS, FOX_HEAD_DIM), 1.0),
        "cache_v": nrm((DEPTH, n_phys, PAGE_SIZE, FOX_HEADS, FOX_HEAD_DIM), 1.0),
        "cache_logf": jax.nn.log_sigmoid(FORGET_BIAS + nrm((DEPTH, n_phys, PAGE_SIZE, FOX_HEADS), 1.0)),
        "state_conv": nrm((DEPTH, DEC_BATCH, CONV_K - 1, CONV_WIDTH), 1.0),
        "state_ssm_re": nrm((DEPTH, DEC_BATCH, SSM_GROUPS, SSM_STATE), 0.1),
        "state_ssm_im": nrm((DEPTH, DEC_BATCH, SSM_GROUPS, SSM_STATE), 0.1),
        "page_table": page_table,
        "norm_mix": 1.0 + nrm((DEPTH, D_MODEL), 0.02),
        "w_in": nrm((DEPTH, D_MODEL, N_IN), D_MODEL ** -0.5),
        "conv_w": nrm((DEPTH, CONV_WIDTH, CONV_K), CONV_K ** -0.5),
        "fox_bias": FORGET_BIAS + nrm((DEPTH, FOX_HEADS), 0.1),
        "ssm_a_re": -0.5 + nrm((DEPTH, SSM_GROUPS, SSM_STATE), 0.01),
        "ssm_a_im": jnp.pi * jnp.arange(SSM_STATE, dtype=f32) + nrm((DEPTH, SSM_GROUPS, SSM_STATE), 0.01),
        "ssm_log_dt": log_dt,
        "ssm_b_re": nrm((DEPTH, SSM_GROUPS, SSM_STATE, SSM_GROUP), (2 * SSM_GROUP) ** -0.5),
        "ssm_b_im": nrm((DEPTH, SSM_GROUPS, SSM_STATE, SSM_GROUP), (2 * SSM_GROUP) ** -0.5),
        "ssm_c_re": nrm((DEPTH, SSM_GROUPS, SSM_GROUP, SSM_STATE), SSM_STATE ** -0.5),
        "ssm_c_im": nrm((DEPTH, SSM_GROUPS, SSM_GROUP, SSM_STATE), SSM_STATE ** -0.5),
        "ssm_d": nrm((DEPTH, SSM_WIDTH), 1.0),
        "w_glu": nrm((DEPTH, SSM_WIDTH, SSM_WIDTH), SSM_WIDTH ** -0.5),
        "w_branch": w_branch,
        "w_out": nrm((DEPTH, D_MODEL, D_MODEL), D_MODEL ** -0.5),
        "norm_ffn": 1.0 + nrm((DEPTH, D_MODEL), 0.02),
        "w_up": nrm((DEPTH, D_MODEL, 2 * D_FF), D_MODEL ** -0.5),
        "w_down": nrm((DEPTH, D_FF, D_MODEL), D_FF ** -0.5),
        "norm_final": 1.0 + nrm((D_MODEL,), 0.02),
    }


def reference(x_prompt, x_sample, cache_k, cache_v, cache_logf, state_conv, state_ssm_re,
              state_ssm_im, page_table, norm_mix, w_in, conv_w, fox_bias, ssm_a_re, ssm_a_im,
              ssm_log_dt, ssm_b_re, ssm_b_im, ssm_c_re, ssm_c_im, ssm_d, w_glu, w_branch,
              w_out, norm_ffn, w_up, w_down, norm_final):
    n_pages = PAST_LEN // PAGE_SIZE
    bp = x_prompt.shape[0]
    bd = x_sample.shape[0]
    xp, xs = x_prompt, x_sample
    outs_p = [[] for _ in range(6)]
    outs_s = [[] for _ in range(6)]
    for l in range(DEPTH):
        lw = (norm_mix[l], w_in[l], conv_w[l], fox_bias[l], ssm_a_re[l], ssm_a_im[l],
              ssm_log_dt[l], ssm_b_re[l], ssm_b_im[l], ssm_c_re[l], ssm_c_im[l], ssm_d[l],
              w_glu[l], w_branch[l], w_out[l], norm_ffn[l], w_up[l], w_down[l])
        zc = jnp.zeros((bp, CONV_K - 1, CONV_WIDTH), xp.dtype)
        zh = jnp.zeros((bp, SSM_GROUPS, SSM_STATE), xp.dtype)
        xp, *st = trunk_layer(xp, zc, zh, zh, fox_prompt, *lw)
        for lst, a in zip(outs_p, st):
            lst.append(a)
        k_past = cache_k[l][page_table].reshape(bd, n_pages * PAGE_SIZE, FOX_HEADS, FOX_HEAD_DIM)
        v_past = cache_v[l][page_table].reshape(bd, n_pages * PAGE_SIZE, FOX_HEADS, FOX_HEAD_DIM)
        lf_past = cache_logf[l][page_table].reshape(bd, n_pages * PAGE_SIZE, FOX_HEADS)
        attend = functools.partial(fox_sample, k_past=k_past, v_past=v_past, logf_past=lf_past)
        xs, *st = trunk_layer(xs, state_conv[l], state_ssm_re[l], state_ssm_im[l], attend, *lw)
        for lst, a in zip(outs_s, st):
            lst.append(a)
    k_p, v_p, lf_p, conv_p, sre_p, sim_p = [jnp.stack(a, axis=0) for a in outs_p]
    k_s, v_s, lf_s, conv_s, sre_s, sim_s = [jnp.stack(a, axis=0) for a in outs_s]
    y_prompt = rmsnorm(xp, norm_final)
    y_sample = rmsnorm(xs, norm_final)
    return (y_prompt, y_sample, k_p, v_p, lf_p, conv_p, sre_p, sim_p,
            k_s, v_s, lf_s, conv_s, sre_s, sim_s)
```

```python
import functools
import math

import jax
import jax.numpy as jnp
from jax import lax
from jax.experimental import pallas as pl
from jax.experimental.pallas import tpu as pltpu

F32 = jnp.float32
BF16 = jnp.bfloat16
HIGHEST = lax.Precision.HIGHEST

RMS_EPS = 1e-6
LANES = 128
SUBLANES = 8
VMEM_LIMIT = 56 * 1024 * 1024
S5_CHUNK = 8
NEG_BIG = -1e30


def _params(*sem):
    return pltpu.CompilerParams(dimension_semantics=sem, vmem_limit_bytes=VMEM_LIMIT)


def _sigmoid(x):
    return 1.0 / (1.0 + jnp.exp(-x))


def _log_sigmoid(x):
    t = -x
    return -(jnp.maximum(t, 0.0) + jnp.log1p(jnp.exp(-jnp.abs(t))))


def _gelu_tanh(x):
    c = math.sqrt(2.0 / math.pi)
    return x * (0.5 * (1.0 + jnp.tanh(c * (x + 0.044715 * (x * x * x)))))


def _idiv(x, n):
    assert n & (n - 1) == 0
    return x >> (n.bit_length() - 1)


def _imod(x, n):
    assert n & (n - 1) == 0
    return x & (n - 1)


def _dot(a, b):
    return jnp.dot(a, b, preferred_element_type=F32)


def _dot_f32(a, b):
    return jnp.dot(a, b, preferred_element_type=F32, precision=HIGHEST)


def _rmsnorm_kernel(x_ref, g_ref, o_ref):
    x = x_ref[...]
    ms = jnp.mean(x * x, axis=-1, keepdims=True)
    o_ref[...] = (x * lax.rsqrt(ms + RMS_EPS) * g_ref[...]).astype(o_ref.dtype)


def rmsnorm(x, g, out_dtype):
    m, d = x.shape
    tm = min(256, m)
    return pl.pallas_call(
        _rmsnorm_kernel,
        grid=(m // tm,),
        in_specs=[pl.BlockSpec((tm, d), lambda i: (i, 0)),
                  pl.BlockSpec((1, d), lambda i: (0, 0))],
        out_specs=pl.BlockSpec((tm, d), lambda i: (i, 0)),
        out_shape=jax.ShapeDtypeStruct((m, d), out_dtype),
        compiler_params=_params("parallel"),
    )(x, g.reshape(1, d))


def _mm_kernel(*refs, n_w, epilogue):
    x_ref = refs[0]
    w_refs = refs[1:1 + n_w]
    e_refs = refs[1 + n_w:-1]
    o_ref = refs[-1]
    x = x_ref[...]
    if x.dtype != BF16:
        x = x.astype(BF16)
    accs = [_dot(x, w[...]) for w in w_refs]
    o_ref[...] = epilogue(accs, [e[...] for e in e_refs]).astype(o_ref.dtype)


def matmul(x, ws, n, epilogue, extras, out_dtype, tm, tn):
    m, k = x.shape
    tm = min(tm, m)
    tn = min(tn, n)
    assert m % tm == 0 and n % tn == 0
    in_specs = [pl.BlockSpec((tm, k), lambda i, j: (i, 0))]
    args = [x]
    for w, c0 in ws:
        in_specs.append(pl.BlockSpec((k, tn), lambda i, j, c0=c0: (0, c0 + j)))
        args.append(w)
    for a, kind, c0 in extras:
        if kind == "tile":
            in_specs.append(pl.BlockSpec((tm, tn), lambda i, j, c0=c0: (i, c0 + j)))
        else:
            in_specs.append(pl.BlockSpec((1, tn), lambda i, j, c0=c0: (0, c0 + j)))
        args.append(a)
    return pl.pallas_call(
        functools.partial(_mm_kernel, n_w=len(ws), epilogue=epilogue),
        grid=(m // tm, n // tn),
        in_specs=in_specs,
        out_specs=pl.BlockSpec((tm, tn), lambda i, j: (i, j)),
        out_shape=jax.ShapeDtypeStruct((m, n), out_dtype),
        compiler_params=_params("parallel", "parallel"),
    )(*args)


def _ep_plain(accs, extras):
    return accs[0]


def _ep_sigmoid(accs, extras):
    return _sigmoid(accs[0])


def _ep_logf(accs, extras):
    return _log_sigmoid(accs[0] + extras[0])


def _ep_residual(accs, extras):
    return extras[0] + accs[0]


def _ep_glu(accs, extras):
    return extras[0] * _sigmoid(accs[0])


def _ep_swiglu(accs, extras):
    g = accs[0]
    return (g * _sigmoid(g)) * accs[1]


def _merge_kernel(ya_ref, yb_ref, yc_ref, wa_ref, wb_ref, wc_ref, ga_ref, gb_ref, gc_ref, o_ref):
    a = _dot(ya_ref[...], wa_ref[...])
    b = _dot(yb_ref[...], wb_ref[...])
    c = _dot(yc_ref[...], wc_ref[...])
    o_ref[...] = (ga_ref[...] * a + gb_ref[...] * b + gc_ref[...] * c).astype(o_ref.dtype)


def gated_merge(ya, yb, yc, wa, wb, wc, sg, tm, tn):
    m = ya.shape[0]
    d = wa.shape[1]
    tm = min(tm, m)
    tn = min(tn, d)
    nj = d // tn
    row = lambda a: pl.BlockSpec((tm, a.shape[1]), lambda i, j: (i, 0))
    col = lambda a: pl.BlockSpec((a.shape[0], tn), lambda i, j: (0, j))
    gate = lambda c: pl.BlockSpec((tm, tn), lambda i, j, c=c: (i, c * nj + j))
    return pl.pallas_call(
        _merge_kernel,
        grid=(m // tm, nj),
        in_specs=[row(ya), row(yb), row(yc), col(wa), col(wb), col(wc), gate(0), gate(1), gate(2)],
        out_specs=pl.BlockSpec((tm, tn), lambda i, j: (i, j)),
        out_shape=jax.ShapeDtypeStruct((m, d), BF16),
        compiler_params=_params("parallel", "parallel"),
    )(ya, yb, yc, wa, wb, wc, sg, sg, sg)


def _conv_prompt_kernel(h_ref, b_ref, c_ref, w_ref, y_ref, st_ref):
    u = c_ref[...] * h_ref[...]
    t = u.shape[0]
    row = lax.broadcasted_iota(jnp.int32, u.shape, 0)
    u1 = jnp.where(row >= 1, pltpu.roll(u, 1, 0), 0.0)
    u2 = jnp.where(row >= 2, pltpu.roll(u, 2, 0), 0.0)
    w = w_ref[...]
    conv = u2 * w[0:1, :] + u1 * w[1:2, :] + u * w[2:3, :]
    y_ref[...] = (b_ref[...] * conv).astype(y_ref.dtype)
    st_ref[...] = u[t - 2:t, :]


def conv_prompt(hbc, wt, bsz, t):
    cw = wt.shape[1]
    tc = min(256, cw)
    nc = cw // tc
    hbc3 = hbc.reshape(bsz, t, 3 * cw)
    spec = lambda off: pl.BlockSpec((None, t, tc), lambda b, c, off=off: (b, 0, off * nc + c))
    y, st = pl.pallas_call(
        _conv_prompt_kernel,
        grid=(bsz, nc),
        in_specs=[spec(0), spec(1), spec(2), pl.BlockSpec((3, tc), lambda b, c: (0, c))],
        out_specs=[pl.BlockSpec((None, t, tc), lambda b, c: (b, 0, c)),
                   pl.BlockSpec((None, 2, tc), lambda b, c: (b, 0, c))],
        out_shape=[jax.ShapeDtypeStruct((bsz, t, cw), BF16),
                   jax.ShapeDtypeStruct((bsz, 2, cw), F32)],
        compiler_params=_params("parallel", "parallel"),
    )(hbc3, hbc3, hbc3, wt)
    return y.reshape(bsz * t, cw), st


def _conv_sample_kernel(h_ref, b_ref, c_ref, buf0_ref, buf1_ref, w_ref, y_ref, u_ref):
    u = c_ref[...] * h_ref[...]
    w = w_ref[...]
    conv = buf0_ref[...] * w[0:1, :] + buf1_ref[...] * w[1:2, :] + u * w[2:3, :]
    y_ref[...] = (b_ref[...] * conv).astype(y_ref.dtype)
    u_ref[...] = u


def conv_sample(hbc, buf0, buf1, wt):
    m, cw = buf0.shape
    spec = lambda off: pl.BlockSpec((m, cw), lambda i, off=off: (0, off))
    return pl.pallas_call(
        _conv_sample_kernel,
        grid=(1,),
        in_specs=[spec(0), spec(1), spec(2), spec(0), spec(0), pl.BlockSpec((3, cw), lambda i: (0, 0))],
        out_specs=[spec(0), spec(0)],
        out_shape=[jax.ShapeDtypeStruct((m, cw), BF16), jax.ShapeDtypeStruct((m, cw), F32)],
        compiler_params=_params("arbitrary"),
    )(hbc, hbc, hbc, buf0, buf1, wt)


def _fcum_kernel(lf_ref, f_ref, ft_ref):
    x = lf_ref[...]
    t = x.shape[0]
    row = lax.broadcasted_iota(jnp.int32, x.shape, 0)
    s = 1
    while s < t:
        x = x + jnp.where(row >= s, pltpu.roll(x, s, 0), 0.0)
        s *= 2
    f_ref[...] = x
    ft_ref[...] = x.T


def forget_cumsum(logf, bsz, t):
    lf3 = logf.reshape(bsz, t, LANES)
    return pl.pallas_call(
        _fcum_kernel,
        grid=(bsz,),
        in_specs=[pl.BlockSpec((None, t, LANES), lambda b: (b, 0, 0))],
        out_specs=[pl.BlockSpec((None, t, LANES), lambda b: (b, 0, 0)),
                   pl.BlockSpec((None, LANES, t), lambda b: (b, 0, 0))],
        out_shape=[jax.ShapeDtypeStruct((bsz, t, LANES), F32),
                   jax.ShapeDtypeStruct((bsz, LANES, t), F32)],
        compiler_params=_params("parallel"),
    )(lf3)


def _fox_prompt_kernel(q_ref, k_ref, v_ref, f_ref, ft_ref, o_ref, m_sc, l_sc, acc_sc,
                       *, heads, dh, scale, tq, tk):
    qi = pl.program_id(1)
    ki = pl.program_id(2)

    @pl.when(ki == 0)
    def _():
        m_sc[...] = jnp.full(m_sc.shape, NEG_BIG, F32)
        l_sc[...] = jnp.zeros(l_sc.shape, F32)
        acc_sc[...] = jnp.zeros(acc_sc.shape, F32)

    @pl.when(ki <= qi)
    def _():
        qpos = qi * tq + lax.broadcasted_iota(jnp.int32, (tq, tk), 0)
        kpos = ki * tk + lax.broadcasted_iota(jnp.int32, (tq, tk), 1)
        causal = kpos <= qpos
        fq = f_ref[...]
        fk = ft_ref[...]
        for h in range(heads):
            sl = slice(h * dh, (h + 1) * dh)
            qh = q_ref[:, sl]
            kh = k_ref[:, sl].astype(BF16)
            vh = v_ref[:, sl].astype(BF16)
            s = lax.dot_general(qh, kh, (((1,), (1,)), ((), ())), preferred_element_type=F32) * scale
            s = s + (fq[:, h:h + 1] - fk[h:h + 1, :])
            s = jnp.where(causal, s, NEG_BIG)
            m_prev = m_sc[h]
            m_new = jnp.maximum(m_prev, jnp.max(s, axis=1, keepdims=True))
            alpha = jnp.exp(m_prev - m_new)
            p = jnp.exp(s - jnp.tile(m_new, (1, tk // LANES)))
            l_sc[h] = alpha * l_sc[h] + jnp.sum(p, axis=1, keepdims=True)
            acc_sc[h] = alpha * acc_sc[h] + _dot(p.astype(BF16), vh)
            m_sc[h] = m_new

    @pl.when(ki == qi)
    def _():
        for h in range(heads):
            o_ref[:, h * dh:(h + 1) * dh] = (acc_sc[h] / l_sc[h]).astype(o_ref.dtype)


def fox_prompt(q, k, v, f, ft, bsz, t, heads, dh):
    assert dh == LANES
    tq = tk = min(256, t)
    nq = t // tq
    width = heads * dh
    hp = min(LANES, -(-heads // SUBLANES) * SUBLANES)
    kernel = functools.partial(_fox_prompt_kernel, heads=heads, dh=dh, scale=dh ** -0.5, tq=tq, tk=tk)
    return pl.pallas_call(
        kernel,
        grid=(bsz, nq, nq),
        in_specs=[
            pl.BlockSpec((tq, width), lambda b, i, j: (b * nq + i, 0)),
            pl.BlockSpec((tk, width), lambda b, i, j: (b * nq + jnp.minimum(i, j), 0)),
            pl.BlockSpec((tk, width), lambda b, i, j: (b * nq + jnp.minimum(i, j), 0)),
            pl.BlockSpec((None, tq, LANES), lambda b, i, j: (b, i, 0)),
            pl.BlockSpec((None, hp, tk), lambda b, i, j: (b, 0, jnp.minimum(i, j))),
        ],
        out_specs=pl.BlockSpec((tq, width), lambda b, i, j: (b * nq + i, 0)),
        out_shape=jax.ShapeDtypeStruct((bsz * t, width), BF16),
        scratch_shapes=[pltpu.VMEM((heads, tq, LANES), F32),
                        pltpu.VMEM((heads, tq, LANES), F32),
                        pltpu.VMEM((heads, tq, dh), F32)],
        compiler_params=_params("parallel", "parallel", "arbitrary"),
    )(q, k, v, f, ft)


def _after_kernel(pt_ref, lf_ref, o_ref, carry_sc):
    @pl.when(pl.program_id(1) == 0)
    def _():
        carry_sc[...] = jnp.zeros(carry_sc.shape, F32)

    lf = lf_ref[...]
    ps = lf.shape[0]
    r = lax.broadcasted_iota(jnp.int32, (ps, ps), 0)
    c = lax.broadcasted_iota(jnp.int32, (ps, ps), 1)
    later = jnp.where(c > r, 1.0, 0.0).astype(F32)
    o_ref[...] = _dot_f32(later, lf) + carry_sc[...]
    carry_sc[...] = carry_sc[...] + jnp.sum(lf, axis=0, keepdims=True)


def forget_after(page_table, cache_logf, layer):
    db, npg = page_table.shape
    _, _, ps, heads = cache_logf.shape
    grid_spec = pltpu.PrefetchScalarGridSpec(
        num_scalar_prefetch=1,
        grid=(db, npg),
        in_specs=[pl.BlockSpec((None, None, ps, heads),
                               lambda b, p, pt: (layer, pt[b, npg - 1 - p], 0, 0))],
        out_specs=pl.BlockSpec((None, None, ps, heads), lambda b, p, pt: (b, npg - 1 - p, 0, 0)),
        scratch_shapes=[pltpu.VMEM((1, heads), F32)],
    )
    return pl.pallas_call(
        _after_kernel,
        grid_spec=grid_spec,
        out_shape=jax.ShapeDtypeStruct((db, npg, ps, heads), F32),
        compiler_params=_params("parallel", "arbitrary"),
    )(page_table, cache_logf)


def _lane_class_reduce(x, op, period):
    s = period
    while s < LANES:
        x = op(x, pltpu.roll(x, s, 1))
        s *= 2
    return x


def _lane_to_col(x, heads):
    sub = lax.broadcasted_iota(jnp.int32, (heads, LANES), 0)
    lane = lax.broadcasted_iota(jnp.int32, (heads, LANES), 1)
    return jnp.sum(jnp.where(lane == sub, jnp.broadcast_to(x, (heads, LANES)), 0.0), axis=1, keepdims=True)


def _fox_sample_kernel(pt_ref, q_ref, kn_ref, vn_ref, fn_ref, aft_ref, k_ref, v_ref, o_ref,
                       m_sc, l_sc, acc_sc, *, scale, heads):
    p = pl.program_id(1)
    ps, _, dh = k_ref.shape
    n = ps * heads
    nb = n // LANES

    @pl.when(p == 0)
    def _():
        m_sc[...] = jnp.full(m_sc.shape, NEG_BIG, F32)
        l_sc[...] = jnp.zeros(l_sc.shape, F32)
        acc_sc[...] = jnp.zeros(acc_sc.shape, F32)

    q = q_ref[...]
    k2 = k_ref[...].reshape(n, dh).astype(BF16)
    v2 = v_ref[...].reshape(n, dh).astype(BF16)
    st = lax.dot_general(q, k2, (((1,), (1,)), ((), ())), preferred_element_type=F32)
    sub = lax.broadcasted_iota(jnp.int32, (heads, n), 0)
    lane = lax.broadcasted_iota(jnp.int32, (heads, n), 1)
    own = _imod(lane, heads) == sub
    s = jnp.sum(jnp.where(own, st, 0.0), axis=0, keepdims=True) * scale
    s = s + fn_ref[...] + aft_ref[...]

    m_page = s[:, 0:LANES]
    for j in range(1, nb):
        m_page = jnp.maximum(m_page, s[:, j * LANES:(j + 1) * LANES])
    m_page = _lane_class_reduce(m_page, jnp.maximum, heads)
    m_prev = m_sc[...]
    m_new = jnp.maximum(m_prev, m_page)
    alpha = jnp.exp(m_prev - m_new)
    pr = jnp.exp(s - jnp.tile(m_new, (1, nb)))
    l_page = pr[:, 0:LANES]
    for j in range(1, nb):
        l_page = l_page + pr[:, j * LANES:(j + 1) * LANES]
    l_sc[...] = alpha * l_sc[...] + l_page
    m_sc[...] = m_new
    pm = jnp.where(own, jnp.broadcast_to(pr, (heads, n)), 0.0).astype(BF16)
    acc_sc[...] = acc_sc[...] * _lane_to_col(alpha, heads) + _dot(pm, v2)

    @pl.when(p == pl.num_programs(1) - 1)
    def _():
        s_new = jnp.sum(q.astype(F32) * kn_ref[...], axis=1, keepdims=True) * scale
        m_col = _lane_to_col(m_sc[...], heads)
        l_col = _lane_to_col(_lane_class_reduce(l_sc[...], jnp.add, heads), heads)
        m_fin = jnp.maximum(m_col, s_new)
        a_col = jnp.exp(m_col - m_fin)
        p_new = jnp.exp(s_new - m_fin)
        num = acc_sc[...] * a_col + p_new * vn_ref[...]
        o_ref[...] = (num / (l_col * a_col + p_new)).astype(o_ref.dtype)


def fox_sample(page_table, q, kn, vn, fn_row, after_row, cache_k, cache_v, layer):
    db, npg = page_table.shape
    _, _, ps, heads, dh = cache_k.shape
    n = ps * heads
    assert LANES % heads == 0 and n % LANES == 0
    tok = pl.BlockSpec((None, heads, dh), lambda b, p, pt: (b, 0, 0))
    page = pl.BlockSpec((None, None, ps, heads, dh), lambda b, p, pt: (layer, pt[b, p], 0, 0, 0))
    grid_spec = pltpu.PrefetchScalarGridSpec(
        num_scalar_prefetch=1,
        grid=(db, npg),
        in_specs=[tok, tok, tok,
                  pl.BlockSpec((None, 1, n), lambda b, p, pt: (b, 0, 0)),
                  pl.BlockSpec((None, None, 1, n), lambda b, p, pt: (b, p, 0, 0)),
                  page, page],
        out_specs=tok,
        scratch_shapes=[pltpu.VMEM((1, LANES), F32), pltpu.VMEM((1, LANES), F32),
                        pltpu.VMEM((heads, dh), F32)],
    )
    return pl.pallas_call(
        functools.partial(_fox_sample_kernel, scale=dh ** -0.5, heads=heads),
        grid_spec=grid_spec,
        out_shape=jax.ShapeDtypeStruct((db, heads, dh), BF16),
        compiler_params=_params("parallel", "arbitrary"),
    )(page_table, q, kn, vn, fn_row, after_row, cache_k, cache_v)


def _s5_discretise(a_re, a_im, log_dt):
    dt = jnp.exp(log_dt)
    mag = jnp.exp(a_re * dt)
    ang = a_im * dt
    abar_re = mag * jnp.cos(ang)
    abar_im = mag * jnp.sin(ang)
    den = a_re * a_re + a_im * a_im
    num_re = abar_re - 1.0
    zoh_re = (num_re * a_re + abar_im * a_im) / den
    zoh_im = (abar_im * a_re - num_re * a_im) / den
    return abar_re, abar_im, zoh_re, zoh_im


def _cmul(ar, ai, br, bi):
    return ar * br - ai * bi, ar * bi + ai * br


def _s5_tables_kernel(ar_r, ai_r, ld_r, br_r, bi_r, ar_t, ai_t, ld_t, cr_t, ci_t, ar_f, ai_f, ld_f,
                      kt_ref, wt_ref, et_ref, w0_ref, e0_ref, ap_ref, *, chunk, gsz, psz):
    ngrp = LANES // gsz
    pw = ngrp * psz
    abr, abi, zr, zi = _s5_discretise(ar_r[...], ai_r[...], ld_r[...])
    bbr, bbi = _cmul(zr, zi, br_r[...], bi_r[...])
    atr, ati, _, _ = _s5_discretise(ar_t[...], ai_t[...], ld_t[...])
    ctr, cti = cr_t[...], ci_t[...]
    afr, afi, _, _ = _s5_discretise(ar_f[...], ai_f[...], ld_f[...])

    def group_of(shape, axis, size):
        return _idiv(lax.broadcasted_iota(jnp.int32, shape, axis), size)

    same_kk = group_of((LANES, LANES), 0, gsz) == group_of((LANES, LANES), 1, gsz)
    same_wp = group_of((LANES, pw), 0, gsz) == group_of((LANES, pw), 1, psz)
    same_pe = group_of((pw, LANES), 0, psz) == group_of((pw, LANES), 1, gsz)
    rep = jnp.where(_imod(lax.broadcasted_iota(jnp.int32, (psz, pw), 1), psz)
                    == lax.broadcasted_iota(jnp.int32, (psz, pw), 0), 1.0, 0.0).astype(F32)

    def w_table(wr, wi):
        return jnp.concatenate([jnp.where(same_wp, _dot_f32(wr, rep), 0.0),
                                jnp.where(same_wp, _dot_f32(wi, rep), 0.0)], axis=1)

    def e_table(qr, qi):
        return jnp.concatenate([jnp.where(same_pe, jnp.tile(qr, (ngrp, 1)), 0.0),
                                jnp.where(same_pe, jnp.tile(-qi, (ngrp, 1)), 0.0)], axis=0)

    w0_ref[...] = w_table(bbr, bbi)
    e0_ref[...] = e_table(ctr, cti)
    ap_ref[0:1, :] = jnp.concatenate([afr, afi], axis=1)

    pr_r, pi_r = jnp.ones_like(abr), jnp.zeros_like(abr)
    pr_t, pi_t = jnp.ones_like(atr), jnp.zeros_like(atr)
    pr_f, pi_f = jnp.ones_like(afr), jnp.zeros_like(afr)
    for tau in range(chunk):
        wr, wi = _cmul(pr_r, pi_r, bbr, bbi)
        kt_ref[tau] = jnp.where(same_kk, _dot_f32(wr, ctr) - _dot_f32(wi, cti), 0.0).astype(kt_ref.dtype)
        wt_ref[tau] = w_table(wr, wi).astype(wt_ref.dtype)
        pr_r, pi_r = _cmul(pr_r, pi_r, abr, abi)
        pr_t, pi_t = _cmul(pr_t, pi_t, atr, ati)
        pr_f, pi_f = _cmul(pr_f, pi_f, afr, afi)
        qr, qi = _cmul(ctr, cti, pr_t, pi_t)
        et_ref[tau] = e_table(qr, qi).astype(et_ref.dtype)
    ap_ref[1:2, :] = jnp.concatenate([pr_f, pi_f], axis=1)


def s5_tables(a_re, a_im, log_dt, b_re, b_im, c_re, c_im):
    g, p = a_re.shape
    c = b_re.shape[2]
    assert LANES % c == 0 and (g * c) % LANES == 0
    nl = g * c // LANES
    ngrp = LANES // c
    pw = ngrp * p
    chunk = S5_CHUNK
    ld = jnp.broadcast_to(log_dt[:, None], (g, p))
    rows = lambda a: jnp.repeat(a, c, axis=0)
    lanes = lambda a: jnp.repeat(a.T, c, axis=1)
    flat = lambda a: a.reshape(1, g * p)
    args = [rows(a_re), rows(a_im), rows(ld),
            b_re.transpose(0, 2, 1).reshape(g * c, p), b_im.transpose(0, 2, 1).reshape(g * c, p),
            lanes(a_re), lanes(a_im), lanes(ld),
            c_re.transpose(2, 0, 1).reshape(p, g * c), c_im.transpose(2, 0, 1).reshape(p, g * c),
            flat(a_re), flat(a_im), flat(ld)]
    spec_r = pl.BlockSpec((LANES, p), lambda i: (i, 0))
    spec_t = pl.BlockSpec((p, LANES), lambda i: (0, i))
    spec_f = pl.BlockSpec((1, pw), lambda i: (0, i))
    return pl.pallas_call(
        functools.partial(_s5_tables_kernel, chunk=chunk, gsz=c, psz=p),
        grid=(nl,),
        in_specs=[spec_r] * 5 + [spec_t] * 5 + [spec_f] * 3,
        out_specs=[pl.BlockSpec((None, chunk, LANES, LANES), lambda i: (i, 0, 0, 0)),
                   pl.BlockSpec((None, chunk, LANES, 2 * pw), lambda i: (i, 0, 0, 0)),
                   pl.BlockSpec((None, chunk, 2 * pw, LANES), lambda i: (i, 0, 0, 0)),
                   pl.BlockSpec((None, LANES, 2 * pw), lambda i: (i, 0, 0)),
                   pl.BlockSpec((None, 2 * pw, LANES), lambda i: (i, 0, 0)),
                   pl.BlockSpec((None, 2, 2 * pw), lambda i: (i, 0, 0))],
        out_shape=[jax.ShapeDtypeStruct((nl, chunk, LANES, LANES), BF16),
                   jax.ShapeDtypeStruct((nl, chunk, LANES, 2 * pw), BF16),
                   jax.ShapeDtypeStruct((nl, chunk, 2 * pw, LANES), BF16),
                   jax.ShapeDtypeStruct((nl, LANES, 2 * pw), F32),
                   jax.ShapeDtypeStruct((nl, 2 * pw, LANES), F32),
                   jax.ShapeDtypeStruct((nl, 2, 2 * pw), F32)],
        compiler_params=_params("parallel"),
    )(*args)


def _s5_prompt_kernel(u_ref, d_ref, kt_ref, wt_ref, et_ref, ap_ref, z_ref, hre_ref, him_ref, s_sc,
                      *, chunk, bsz):
    m = u_ref.shape[0]
    rows = m // chunk
    nchunk = rows // bsz
    pw = ap_ref.shape[1] // 2
    u = [u_ref[pl.ds(i, rows, stride=chunk), :] for i in range(chunk)]
    ub = [x.astype(BF16) for x in u]
    s = _dot(ub[0], wt_ref[chunk - 1])
    for i in range(1, chunk):
        s = s + _dot(ub[i], wt_ref[chunk - 1 - i])
    s_sc[...] = s
    a_r = ap_ref[1:2, 0:pw]
    a_i = ap_ref[1:2, pw:2 * pw]
    for b in range(bsz):
        def body(g8, carry):
            hr, hi = carry
            base = pl.multiple_of(b * nchunk + g8 * SUBLANES, SUBLANES)
            tile = s_sc[pl.ds(base, SUBLANES), :]
            before_r, before_i = [], []
            for r in range(SUBLANES):
                before_r.append(hr)
                before_i.append(hi)
                nr = a_r * hr - a_i * hi + tile[r:r + 1, 0:pw]
                ni = a_r * hi + a_i * hr + tile[r:r + 1, pw:2 * pw]
                hr, hi = nr, ni
            s_sc[pl.ds(base, SUBLANES), 0:pw] = jnp.concatenate(before_r, axis=0)
            s_sc[pl.ds(base, SUBLANES), pw:2 * pw] = jnp.concatenate(before_i, axis=0)
            return hr, hi

        zero = jnp.zeros((1, pw), F32)
        hr, hi = lax.fori_loop(0, nchunk // SUBLANES, body, (zero, zero))
        hre_ref[b:b + 1, :] = hr
        him_ref[b:b + 1, :] = hi
    h_in = s_sc[...].astype(BF16)
    d = d_ref[...]
    for i in range(chunk):
        y = _dot(h_in, et_ref[i])
        for j in range(i + 1):
            y = y + _dot(ub[j], kt_ref[i - j])
        y = y + d * u[i]
        z_ref[pl.ds(i, rows, stride=chunk), :] = _gelu_tanh(y)


def s5_prompt(u, d_skip, kt, wt, et, ap, bsz):
    m, width = u.shape
    nl = width // LANES
    chunk = kt.shape[1]
    pw2 = ap.shape[2]
    rows = m // chunk
    assert (m // bsz) % (chunk * SUBLANES) == 0
    col = pl.BlockSpec((m, LANES), lambda i: (0, i))
    tab = lambda a: pl.BlockSpec((None,) + a.shape[1:], lambda i: (i,) + (0,) * (a.ndim - 1))
    st = pl.BlockSpec((bsz, pw2 // 2), lambda i: (0, i))
    return pl.pallas_call(
        functools.partial(_s5_prompt_kernel, chunk=chunk, bsz=bsz),
        grid=(nl,),
        in_specs=[col, pl.BlockSpec((1, LANES), lambda i: (0, i)), tab(kt), tab(wt), tab(et), tab(ap)],
        out_specs=[col, st, st],
        out_shape=[jax.ShapeDtypeStruct((m, width), F32),
                   jax.ShapeDtypeStruct((bsz, nl * pw2 // 2), F32),
                   jax.ShapeDtypeStruct((bsz, nl * pw2 // 2), F32)],
        scratch_shapes=[pltpu.VMEM((rows, pw2), F32)],
        compiler_params=_params("parallel"),
    )(u, d_skip.reshape(1, width), kt, wt, et, ap)


def _s5_sample_kernel(u_ref, d_ref, hre_ref, him_ref, w0_ref, e0_ref, ap_ref, z_ref, ore_ref, oim_ref):
    pw = hre_ref.shape[1]
    u = u_ref[...]
    bu = _dot_f32(u, w0_ref[...])
    a_r = ap_ref[0:1, 0:pw]
    a_i = ap_ref[0:1, pw:2 * pw]
    hr, hi = hre_ref[...], him_ref[...]
    nr = a_r * hr - a_i * hi + bu[:, 0:pw]
    ni = a_r * hi + a_i * hr + bu[:, pw:2 * pw]
    y = _dot_f32(jnp.concatenate([nr, ni], axis=1), e0_ref[...]) + d_ref[...] * u
    z_ref[...] = _gelu_tanh(y)
    ore_ref[...] = nr
    oim_ref[...] = ni


def s5_sample(u, d_skip, h_re, h_im, w0, e0, ap):
    m, width = u.shape
    nl = width // LANES
    pw2 = ap.shape[2]
    col = pl.BlockSpec((m, LANES), lambda i: (0, i))
    tab = lambda a: pl.BlockSpec((None,) + a.shape[1:], lambda i: (i,) + (0,) * (a.ndim - 1))
    st = pl.BlockSpec((m, pw2 // 2), lambda i: (0, i))
    return pl.pallas_call(
        _s5_sample_kernel,
        grid=(nl,),
        in_specs=[col, pl.BlockSpec((1, LANES), lambda i: (0, i)), st, st, tab(w0), tab(e0), tab(ap)],
        out_specs=[col, st, st],
        out_shape=[jax.ShapeDtypeStruct((m, width), F32),
                   jax.ShapeDtypeStruct(h_re.shape, F32),
                   jax.ShapeDtypeStruct(h_im.shape, F32)],
        compiler_params=_params("parallel"),
    )(u, d_skip.reshape(1, width), h_re, h_im, w0, e0, ap)


def _layer(x, lw, dims, prompt, sample_state):
    cw, fw, sw, heads, dh, d_ff = dims
    m, d = x.shape
    big = m >= 1024
    tm = 1024 if big else m
    xn = rmsnorm(x, lw["norm_mix"], BF16)
    w_in = lw["w_in"]
    tn = min(1024, cw)
    assert fw % tn == 0 and sw % tn == 0 and d % tn == 0
    nb = lambda cols: cols // tn
    o_q, o_k, o_v, o_u, o_g = nb(3 * cw), nb(3 * cw + fw), nb(3 * cw + 2 * fw), nb(3 * cw + 3 * fw), nb(3 * cw + 3 * fw + sw)
    proj = lambda c0, n, ep, dt, extras=(): matmul(xn, [(w_in, c0)], n, ep, list(extras), dt, tm, tn)
    hbc = proj(0, 3 * cw, _ep_plain, F32)
    q = proj(o_q, fw, _ep_plain, BF16)
    k = proj(o_k, fw, _ep_plain, F32)
    v = proj(o_v, fw, _ep_plain, F32)
    u_c = proj(o_u, sw, _ep_plain, F32)
    sg = proj(o_g, 3 * d, _ep_sigmoid, F32)
    logf = matmul(xn, [(lw["w_f"], 0)], LANES, _ep_logf, [(lw["fox_bias"], "row", 0)], F32, tm, LANES)

    kt, wt, et, w0, e0, ap = lw["s5_tables"]
    if prompt is not None:
        bsz, t = prompt
        y_a, conv_new = conv_prompt(hbc, lw["conv_wt"], bsz, t)
        f, ft = forget_cumsum(logf, bsz, t)
        y_b = fox_prompt(q, k, v, f, ft, bsz, t, heads, dh)
        z, h_re, h_im = s5_prompt(u_c, lw["ssm_d"], kt, wt, et, ap, bsz)
    else:
        st = sample_state
        buf = st["conv"]
        y_a, u_a = conv_sample(hbc, buf[:, 0], buf[:, 1], lw["conv_wt"])
        conv_new = jnp.stack([buf[:, 1], u_a], axis=1)
        lf_new = logf[:, :heads]
        ps = st["cache_k"].shape[2]
        after = forget_after(st["page_table"], st["cache_logf"], st["layer"])
        db, npg = st["page_table"].shape
        y_b = fox_sample(st["page_table"], q.reshape(m, heads, dh), k.reshape(m, heads, dh),
                         v.reshape(m, heads, dh), jnp.tile(lf_new, (1, ps)).reshape(db, 1, ps * heads),
                         after.reshape(db, npg, 1, ps * heads), st["cache_k"], st["cache_v"], st["layer"])
        y_b = y_b.reshape(m, fw)
        z, h_re, h_im = s5_sample(u_c, lw["ssm_d"], st["h_re"], st["h_im"], w0, e0, ap)
    y_c = matmul(z, [(lw["w_glu"], 0)], sw, _ep_glu, [(z, "tile", 0)], BF16, tm, tn)

    merged = gated_merge(y_a, y_b, y_c, lw["wa"], lw["wb"], lw["wc"], sg, min(tm, 512), 1024)
    x = matmul(merged, [(lw["w_out"], 0)], d, _ep_residual, [(x, "tile", 0)], F32, tm, tn)
    xn2 = rmsnorm(x, lw["norm_ffn"], BF16)
    tf = 256
    hid = matmul(xn2, [(lw["w_up"], 0), (lw["w_up"], d_ff // tf)], d_ff, _ep_swiglu, [], BF16, tm, tf)
    x = matmul(hid, [(lw["w_down"], 0)], d, _ep_residual, [(x, "tile", 0)], F32, min(tm, 512), 512)
    return x, k, v, logf, conv_new, h_re, h_im


def kernel(x_prompt, x_sample, cache_k, cache_v, cache_logf, state_conv, state_ssm_re, state_ssm_im,
           page_table, norm_mix, w_in, conv_w, fox_bias, ssm_a_re, ssm_a_im, ssm_log_dt, ssm_b_re,
           ssm_b_im, ssm_c_re, ssm_c_im, ssm_d, w_glu, w_branch, w_out, norm_ffn, w_up, w_down,
           norm_final):
    bp, t, d = x_prompt.shape
    bd, td, _ = x_sample.shape
    depth = w_in.shape[0]
    cw = conv_w.shape[1]
    heads = fox_bias.shape[1]
    dh = cache_k.shape[4]
    fw = heads * dh
    g, p = ssm_a_re.shape[1:]
    sw = ssm_d.shape[1]
    d_ff = w_down.shape[1]
    assert td == 1
    dims = (cw, fw, sw, heads, dh, d_ff)
    f0 = 3 * cw + 3 * fw

    xp = x_prompt.reshape(bp * t, d)
    xs = x_sample.reshape(bd * td, d)
    outs_p, outs_s = [], []
    for l in range(depth):
        wl = w_in[l]
        lw = {
            "norm_mix": norm_mix[l], "norm_ffn": norm_ffn[l],
            "w_in": jnp.concatenate([wl[:, :f0], wl[:, f0 + heads:]], axis=1).astype(BF16),
            "w_f": jnp.pad(wl[:, f0:f0 + heads], ((0, 0), (0, LANES - heads))).astype(BF16),
            "fox_bias": jnp.pad(fox_bias[l], (0, LANES - heads)).reshape(1, LANES),
            "conv_wt": conv_w[l].T,
            "ssm_d": ssm_d[l],
            "w_glu": w_glu[l].astype(BF16),
            "wa": w_branch[l, :cw].astype(BF16),
            "wb": w_branch[l, cw:cw + fw].astype(BF16),
            "wc": w_branch[l, cw + fw:].astype(BF16),
            "w_out": w_out[l].astype(BF16),
            "w_up": w_up[l].astype(BF16),
            "w_down": w_down[l].astype(BF16),
            "s5_tables": s5_tables(ssm_a_re[l], ssm_a_im[l], ssm_log_dt[l], ssm_b_re[l], ssm_b_im[l],
                                   ssm_c_re[l], ssm_c_im[l]),
        }
        xp, kp, vp, lfp, cvp, hrp, hip = _layer(xp, lw, dims, (bp, t), None)
        outs_p.append((kp.reshape(bp, t, heads, dh), vp.reshape(bp, t, heads, dh),
                       lfp[:, :heads].reshape(bp, t, heads), cvp,
                       hrp.reshape(bp, g, p), hip.reshape(bp, g, p)))
        st = {"conv": state_conv[l], "h_re": state_ssm_re[l].reshape(bd, g * p),
              "h_im": state_ssm_im[l].reshape(bd, g * p), "page_table": page_table,
              "cache_k": cache_k, "cache_v": cache_v, "cache_logf": cache_logf, "layer": l}
        xs, ks, vs, lfs, cvs, hrs, his = _layer(xs, lw, dims, None, st)
        outs_s.append((ks.reshape(bd, td, heads, dh), vs.reshape(bd, td, heads, dh),
                       lfs[:, :heads].reshape(bd, td, heads), cvs,
                       hrs.reshape(bd, g, p), his.reshape(bd, g, p)))
    stack = lambda outs: [jnp.stack(a, axis=0) for a in zip(*outs)]
    y_prompt = rmsnorm(xp, norm_final, F32).reshape(bp, t, d)
    y_sample = rmsnorm(xs, norm_final, F32).reshape(bd, td, d)
    return (y_prompt, y_sample, *stack(outs_p), *stack(outs_s))
```

```python
import functools
import math

import jax
import jax.numpy as jnp
from jax import lax
from jax.experimental import pallas as pl
from jax.experimental.pallas import tpu as pltpu

F32 = jnp.float32
BF16 = jnp.bfloat16
HIGHEST = lax.Precision.HIGHEST

RMS_EPS = 1e-6
LOG2E = math.log2(math.e)
LANES = 128
SUBLANES = 8
VMEM_LIMIT = 56 * 1024 * 1024
S5_CHUNK = 8
NEG_BIG = -1e30
FOX_ROW_CHUNK = 32
ATTN_PAGES_PER_STEP = 4
AFTER_PAGES_PER_STEP = 16


def _params(*sem):
    return pltpu.CompilerParams(dimension_semantics=sem, vmem_limit_bytes=VMEM_LIMIT)


def _sigmoid(x):
    return 1.0 / (1.0 + jnp.exp(-x))


def _log_sigmoid(x):
    t = -x
    return -(jnp.maximum(t, 0.0) + jnp.log1p(jnp.exp(-jnp.abs(t))))


def _gelu_tanh(x):
    c = math.sqrt(2.0 / math.pi)
    return x * (0.5 * (1.0 + jnp.tanh(c * (x + 0.044715 * (x * x * x)))))


def _idiv(x, n):
    assert n & (n - 1) == 0
    return x >> (n.bit_length() - 1)


def _imod(x, n):
    assert n & (n - 1) == 0
    return x & (n - 1)


def _dot(a, b):
    return jnp.dot(a, b, preferred_element_type=F32)


def _dot_f32(a, b):
    return jnp.dot(a, b, preferred_element_type=F32, precision=HIGHEST)


def _rmsnorm_kernel(x_ref, g_ref, o_ref):
    x = x_ref[...]
    ms = jnp.mean(x * x, axis=-1, keepdims=True)
    o_ref[...] = (x * lax.rsqrt(ms + RMS_EPS) * g_ref[...]).astype(o_ref.dtype)


def rmsnorm(x, g, out_dtype):
    m, d = x.shape
    tm = min(256, m)
    return pl.pallas_call(
        _rmsnorm_kernel,
        grid=(m // tm,),
        in_specs=[pl.BlockSpec((tm, d), lambda i: (i, 0)),
                  pl.BlockSpec((1, d), lambda i: (0, 0))],
        out_specs=pl.BlockSpec((tm, d), lambda i: (i, 0)),
        out_shape=jax.ShapeDtypeStruct((m, d), out_dtype),
        compiler_params=_params("parallel"),
    )(x, g.reshape(1, d))


def _mm_kernel(*refs, n_w, epilogue):
    x_ref = refs[0]
    w_refs = refs[1:1 + n_w]
    e_refs = refs[1 + n_w:-1]
    o_ref = refs[-1]
    x = x_ref[...]
    if x.dtype != BF16:
        x = x.astype(BF16)
    accs = [_dot(x, w[...]) for w in w_refs]
    o_ref[...] = epilogue(accs, [e[...] for e in e_refs]).astype(o_ref.dtype)


def matmul(x, ws, n, epilogue, extras, out_dtype, tm, tn):
    m, k = x.shape
    tm = min(tm, m)
    tn = min(tn, n)
    assert m % tm == 0 and n % tn == 0
    in_specs = [pl.BlockSpec((tm, k), lambda i, j: (i, 0))]
    args = [x]
    for w, c0 in ws:
        in_specs.append(pl.BlockSpec((k, tn), lambda i, j, c0=c0: (0, c0 + j)))
        args.append(w)
    for a, kind, c0 in extras:
        if kind == "tile":
            in_specs.append(pl.BlockSpec((tm, tn), lambda i, j, c0=c0: (i, c0 + j)))
        else:
            in_specs.append(pl.BlockSpec((1, tn), lambda i, j, c0=c0: (0, c0 + j)))
        args.append(a)
    return pl.pallas_call(
        functools.partial(_mm_kernel, n_w=len(ws), epilogue=epilogue),
        grid=(m // tm, n // tn),
        in_specs=in_specs,
        out_specs=pl.BlockSpec((tm, tn), lambda i, j: (i, j)),
        out_shape=jax.ShapeDtypeStruct((m, n), out_dtype),
        compiler_params=_params("parallel", "parallel"),
    )(*args)


def _mmw_kernel(*refs, n_w, n_e, cast, epilogue):
    x_ref = refs[0]
    w_refs = refs[1:1 + n_w]
    e_refs = refs[1 + n_w:1 + n_w + n_e]
    if cast:
        o_ref = refs[-1 - n_w]
        wb_refs = refs[-n_w:]

        @pl.when(pl.program_id(1) == 0)
        def _():
            for w, wb in zip(w_refs, wb_refs):
                wb[...] = w[...].astype(BF16)
    else:
        o_ref = refs[-1]
        wb_refs = w_refs
    x = x_ref[...]
    if x.dtype != BF16:
        x = x.astype(BF16)
    accs = [_dot(x, wb[...]) for wb in wb_refs]
    o_ref[...] = epilogue(accs, [e[...] for e in e_refs]).astype(o_ref.dtype)


def matmul_ws(x, ws, n, epilogue, extras, out_dtype, tm, tn, layer=None, stack=None):
    m, k = x.shape
    tm = min(tm, m)
    tn = min(tn, n)
    assert m % tm == 0 and n % tn == 0
    cast = ws[0][0].ndim == 3
    in_specs = [pl.BlockSpec((tm, k), lambda j, i: (i, 0))]
    args = [x]
    for w, c0 in ws:
        if cast:
            in_specs.append(pl.BlockSpec((None, k, tn), lambda j, i, c0=c0: (layer, 0, c0 + j)))
        else:
            in_specs.append(pl.BlockSpec((k, tn), lambda j, i, c0=c0: (0, c0 + j)))
        args.append(w)
    for a, kind, c0 in extras:
        if kind == "tile":
            in_specs.append(pl.BlockSpec((tm, tn), lambda j, i, c0=c0: (i, c0 + j)))
        else:
            in_specs.append(pl.BlockSpec((1, tn), lambda j, i, c0=c0: (0, c0 + j)))
        args.append(a)
    if stack is None:
        out_spec = pl.BlockSpec((tm, tn), lambda j, i: (i, j))
        out_shape = jax.ShapeDtypeStruct((m, n), out_dtype)
        aliases = {}
    else:
        buf, slab = stack
        in_specs.append(pl.BlockSpec(memory_space=pl.ANY))
        args.append(buf)
        out_spec = pl.BlockSpec((None, tm, tn), lambda j, i: (slab, i, j))
        out_shape = jax.ShapeDtypeStruct(buf.shape, buf.dtype)
        aliases = {len(args) - 1: 0}
    return pl.pallas_call(
        functools.partial(_mmw_kernel, n_w=len(ws), n_e=len(extras), cast=cast, epilogue=epilogue),
        grid=(n // tn, m // tm),
        in_specs=in_specs,
        out_specs=out_spec,
        out_shape=out_shape,
        scratch_shapes=[pltpu.VMEM((k, tn), BF16) for _ in ws] if cast else [],
        input_output_aliases=aliases,
        compiler_params=_params("parallel", "arbitrary"),
    )(*args)


def _ep_plain(accs, extras):
    return accs[0]


def _ep_sigmoid(accs, extras):
    return _sigmoid(accs[0])


def _ep_logf(accs, extras):
    return _log_sigmoid(accs[0] + extras[0])


def _ep_residual(accs, extras):
    return extras[0] + accs[0]


def _ep_glu(accs, extras):
    return extras[0] * _sigmoid(accs[0])


def _ep_swiglu(accs, extras):
    g = accs[0]
    return (g * _sigmoid(g)) * accs[1]


def _merge_kernel(ya_ref, yb_ref, yc_ref, w_ref, ga_ref, gb_ref, gc_ref, o_ref, wb_ref):
    @pl.when(pl.program_id(1) == 0)
    def _():
        wb_ref[...] = w_ref[...].astype(BF16)

    ka = ya_ref.shape[1]
    kb = ka + yb_ref.shape[1]
    a = _dot(ya_ref[...], wb_ref[0:ka, :])
    b = _dot(yb_ref[...], wb_ref[ka:kb, :])
    c = _dot(yc_ref[...], wb_ref[kb:, :])
    o_ref[...] = (ga_ref[...] * a + gb_ref[...] * b + gc_ref[...] * c).astype(o_ref.dtype)


def gated_merge(ya, yb, yc, w_branch, layer, sg, tm, tn):
    m = ya.shape[0]
    _, mix, d = w_branch.shape
    tm = min(tm, m)
    tn = min(tn, d)
    nj = d // tn
    row = lambda a: pl.BlockSpec((tm, a.shape[1]), lambda j, i: (i, 0))
    gate = lambda c: pl.BlockSpec((tm, tn), lambda j, i, c=c: (i, c * nj + j))
    return pl.pallas_call(
        _merge_kernel,
        grid=(nj, m // tm),
        in_specs=[row(ya), row(yb), row(yc),
                  pl.BlockSpec((None, mix, tn), lambda j, i: (layer, 0, j)),
                  gate(0), gate(1), gate(2)],
        out_specs=pl.BlockSpec((tm, tn), lambda j, i: (i, j)),
        out_shape=jax.ShapeDtypeStruct((m, d), BF16),
        scratch_shapes=[pltpu.VMEM((mix, tn), BF16)],
        compiler_params=_params("parallel", "arbitrary"),
    )(ya, yb, yc, w_branch, sg, sg, sg)


def _conv_prompt_kernel(h_ref, b_ref, c_ref, w_ref, y_ref, st_ref):
    u = c_ref[...] * h_ref[...]
    t = u.shape[0]
    row = lax.broadcasted_iota(jnp.int32, u.shape, 0)
    u1 = jnp.where(row >= 1, pltpu.roll(u, 1, 0), 0.0)
    u2 = jnp.where(row >= 2, pltpu.roll(u, 2, 0), 0.0)
    w = w_ref[...]
    conv = u2 * w[0:1, :] + u1 * w[1:2, :] + u * w[2:3, :]
    y_ref[...] = (b_ref[...] * conv).astype(y_ref.dtype)
    st_ref[...] = u[t - 2:t, :]


def conv_prompt(hbc, wt, bsz, t):
    cw = wt.shape[1]
    tc = min(256, cw)
    nc = cw // tc
    hbc3 = hbc.reshape(bsz, t, 3 * cw)
    spec = lambda off: pl.BlockSpec((None, t, tc), lambda b, c, off=off: (b, 0, off * nc + c))
    y, st = pl.pallas_call(
        _conv_prompt_kernel,
        grid=(bsz, nc),
        in_specs=[spec(0), spec(1), spec(2), pl.BlockSpec((3, tc), lambda b, c: (0, c))],
        out_specs=[pl.BlockSpec((None, t, tc), lambda b, c: (b, 0, c)),
                   pl.BlockSpec((None, 2, tc), lambda b, c: (b, 0, c))],
        out_shape=[jax.ShapeDtypeStruct((bsz, t, cw), BF16),
                   jax.ShapeDtypeStruct((bsz, 2, cw), F32)],
        compiler_params=_params("parallel", "parallel"),
    )(hbc3, hbc3, hbc3, wt)
    return y.reshape(bsz * t, cw), st


def _conv_sample_kernel(h_ref, b_ref, c_ref, buf0_ref, buf1_ref, w_ref, y_ref, u_ref):
    u = c_ref[...] * h_ref[...]
    w = w_ref[...]
    conv = buf0_ref[...] * w[0:1, :] + buf1_ref[...] * w[1:2, :] + u * w[2:3, :]
    y_ref[...] = (b_ref[...] * conv).astype(y_ref.dtype)
    u_ref[...] = u


def conv_sample(hbc, buf0, buf1, wt):
    m, cw = buf0.shape
    spec = lambda off: pl.BlockSpec((m, cw), lambda i, off=off: (0, off))
    return pl.pallas_call(
        _conv_sample_kernel,
        grid=(1,),
        in_specs=[spec(0), spec(1), spec(2), spec(0), spec(0), pl.BlockSpec((3, cw), lambda i: (0, 0))],
        out_specs=[spec(0), spec(0)],
        out_shape=[jax.ShapeDtypeStruct((m, cw), BF16), jax.ShapeDtypeStruct((m, cw), F32)],
        compiler_params=_params("arbitrary"),
    )(hbc, hbc, hbc, buf0, buf1, wt)


def _fcum_kernel(lf_ref, f_ref, ft_ref):
    x = lf_ref[...]
    t = x.shape[0]
    row = lax.broadcasted_iota(jnp.int32, x.shape, 0)
    s = 1
    while s < t:
        x = x + jnp.where(row >= s, pltpu.roll(x, s, 0), 0.0)
        s *= 2
    f_ref[...] = x
    ft_ref[...] = x.T


def forget_cumsum(logf, bsz, t):
    lf3 = logf.reshape(bsz, t, LANES)
    return pl.pallas_call(
        _fcum_kernel,
        grid=(bsz,),
        in_specs=[pl.BlockSpec((None, t, LANES), lambda b: (b, 0, 0))],
        out_specs=[pl.BlockSpec((None, t, LANES), lambda b: (b, 0, 0)),
                   pl.BlockSpec((None, LANES, t), lambda b: (b, 0, 0))],
        out_shape=[jax.ShapeDtypeStruct((bsz, t, LANES), F32),
                   jax.ShapeDtypeStruct((bsz, LANES, t), F32)],
        compiler_params=_params("parallel"),
    )(lf3)


def _fox_prompt_kernel(q_ref, k_ref, v_ref, f_ref, ft_ref, o_ref, m_sc, acc_sc, fq_sc, s_sc, p_sc, pm_sc,
                       *, heads, dh, scale, tq, tk):
    qi = pl.program_id(1)
    ki = pl.program_id(2)
    rc = FOX_ROW_CHUNK
    nlb = tk // LANES

    @pl.when(ki == 0)
    def _():
        m_sc[...] = jnp.full(m_sc.shape, NEG_BIG, F32)
        acc_sc[...] = jnp.zeros(acc_sc.shape, F32)
        fq = f_ref[...] * LOG2E
        for h in range(heads):
            fq_sc[h] = jnp.broadcast_to(fq[:, h:h + 1], (tq, LANES))

    def block(diagonal):
        fk = ft_ref[...] * LOG2E
        ones = jnp.ones((tk, LANES), BF16)
        ahead = (lax.broadcasted_iota(jnp.int32, (rc, tk), 1) - lax.broadcasted_iota(jnp.int32, (rc, tk), 0))
        for h in range(heads):
            sl = slice(h * dh, (h + 1) * dh)
            buf = h % 2
            kh = k_ref[:, sl].astype(BF16)
            vh = jnp.concatenate([v_ref[:, sl].astype(BF16), ones], axis=1)
            s_sc[buf] = lax.dot_general(q_ref[:, sl], kh, (((1,), (1,)), ((), ())),
                                        preferred_element_type=F32)
            fk_h = fk[h:h + 1, :]
            for r0 in range(0, tq, rc):
                rows = slice(r0, r0 + rc)
                x = s_sc[buf, rows, :] * (scale * LOG2E) + jnp.tile(fq_sc[h, rows, :], (1, nlb)) - fk_h
                if diagonal:
                    x = jnp.where(ahead <= r0, x, NEG_BIG)
                s_sc[buf, rows, :] = x
                pm = x[:, 0:LANES]
                for j in range(1, nlb):
                    pm = jnp.maximum(pm, x[:, j * LANES:(j + 1) * LANES])
                pm_sc[buf, rows, :] = pm
            m_prev = m_sc[h]
            m_new = jnp.maximum(m_prev, jnp.max(pm_sc[buf], axis=1, keepdims=True))
            m_sc[h] = m_new
            pm_sc[buf] = m_new
            alpha = jnp.exp2(m_prev - m_new)
            for r0 in range(0, tq, rc):
                rows = slice(r0, r0 + rc)
                x = s_sc[buf, rows, :] - jnp.tile(pm_sc[buf, rows, :], (1, nlb))
                p_sc[buf, rows, :] = jnp.exp2(x).astype(BF16)
            acc_sc[h] = jnp.tile(alpha, (1, 2)) * acc_sc[h] + _dot(p_sc[buf], vh)

    @pl.when(ki < qi)
    def _():
        block(False)

    @pl.when(ki == qi)
    def _():
        block(True)
        for h in range(heads):
            acc = acc_sc[h]
            o_ref[:, h * dh:(h + 1) * dh] = (acc[:, 0:dh] / acc[:, dh:2 * dh]).astype(o_ref.dtype)


def fox_prompt(q, k, v, layer, f, ft, bsz, t, heads, dh):
    assert dh == LANES
    tq = tk = min(512, t)
    nq = t // tq
    width = heads * dh
    hp = min(LANES, -(-heads // SUBLANES) * SUBLANES)
    kernel = functools.partial(_fox_prompt_kernel, heads=heads, dh=dh, scale=dh ** -0.5, tq=tq, tk=tk)
    kv = pl.BlockSpec((None, tk, width), lambda b, i, j: (layer, b * nq + jnp.minimum(i, j), 0))
    return pl.pallas_call(
        kernel,
        grid=(bsz, nq, nq),
        in_specs=[
            pl.BlockSpec((tq, width), lambda b, i, j: (b * nq + i, 0)),
            kv, kv,
            pl.BlockSpec((None, tq, LANES), lambda b, i, j: (b, i, 0)),
            pl.BlockSpec((None, hp, tk), lambda b, i, j: (b, 0, jnp.minimum(i, j))),
        ],
        out_specs=pl.BlockSpec((tq, width), lambda b, i, j: (b * nq + i, 0)),
        out_shape=jax.ShapeDtypeStruct((bsz * t, width), BF16),
        scratch_shapes=[pltpu.VMEM((heads, tq, LANES), F32),
                        pltpu.VMEM((heads, tq, 2 * dh), F32),
                        pltpu.VMEM((heads, tq, LANES), F32),
                        pltpu.VMEM((2, tq, tk), F32),
                        pltpu.VMEM((2, tq, tk), BF16),
                        pltpu.VMEM((2, tq, LANES), F32)],
        compiler_params=_params("parallel", "parallel", "arbitrary"),
    )(q, k, v, f, ft)


def _after_kernel(pt_ref, *refs, pages):
    lf_refs = refs[:pages]
    o_ref, carry_sc = refs[pages], refs[pages + 1]

    @pl.when(pl.program_id(1) == 0)
    def _():
        carry_sc[...] = jnp.zeros(carry_sc.shape, F32)

    ps = lf_refs[0].shape[0]
    r = lax.broadcasted_iota(jnp.int32, (ps, ps), 0)
    c = lax.broadcasted_iota(jnp.int32, (ps, ps), 1)
    later = jnp.where(c > r, 1.0, 0.0).astype(F32)
    carry = carry_sc[...]
    for s in range(pages):
        lf = lf_refs[s][...]
        o_ref[pages - 1 - s] = _dot_f32(later, lf) + carry
        carry = carry + jnp.sum(lf, axis=0, keepdims=True)
    carry_sc[...] = carry


def forget_after(page_table, cache_logf, layer):
    db, npg = page_table.shape
    _, _, ps, heads = cache_logf.shape
    pages = math.gcd(npg, AFTER_PAGES_PER_STEP)
    nsteps = npg // pages
    lf_spec = lambda s: pl.BlockSpec((None, None, ps, heads),
                                     lambda b, p, pt: (layer, pt[b, npg - 1 - (p * pages + s)], 0, 0))
    grid_spec = pltpu.PrefetchScalarGridSpec(
        num_scalar_prefetch=1,
        grid=(db, nsteps),
        in_specs=[lf_spec(s) for s in range(pages)],
        out_specs=pl.BlockSpec((None, pages, ps, heads), lambda b, p, pt: (b, nsteps - 1 - p, 0, 0)),
        scratch_shapes=[pltpu.VMEM((1, heads), F32)],
    )
    return pl.pallas_call(
        functools.partial(_after_kernel, pages=pages),
        grid_spec=grid_spec,
        out_shape=jax.ShapeDtypeStruct((db, npg, ps, heads), F32),
        compiler_params=_params("parallel", "arbitrary"),
    )(page_table, *([cache_logf] * pages))


def _lane_class_reduce(x, op, period):
    s = period
    while s < LANES:
        x = op(x, pltpu.roll(x, s, 1))
        s *= 2
    return x


def _lane_to_col(x, heads):
    sub = lax.broadcasted_iota(jnp.int32, (heads, LANES), 0)
    lane = lax.broadcasted_iota(jnp.int32, (heads, LANES), 1)
    return jnp.sum(jnp.where(lane == sub, jnp.broadcast_to(x, (heads, LANES)), 0.0), axis=1, keepdims=True)


def _fox_sample_kernel(pt_ref, q_ref, kn_ref, vn_ref, fn_ref, aft_ref, *refs, scale, heads, pages):
    k_refs = refs[:pages]
    v_refs = refs[pages:2 * pages]
    o_ref, m_sc, l_sc, acc_sc = refs[2 * pages:]
    p = pl.program_id(1)
    ps, _, dh = k_refs[0].shape
    npg = ps * heads
    n = pages * npg
    nb = n // LANES

    @pl.when(p == 0)
    def _():
        m_sc[...] = jnp.full(m_sc.shape, NEG_BIG, F32)
        l_sc[...] = jnp.zeros(l_sc.shape, F32)
        acc_sc[...] = jnp.zeros(acc_sc.shape, F32)

    q = q_ref[...]
    k2 = jnp.concatenate([r[...].reshape(npg, dh).astype(BF16) for r in k_refs], axis=0)
    v2 = jnp.concatenate([r[...].reshape(npg, dh).astype(BF16) for r in v_refs], axis=0)
    st = lax.dot_general(q, k2, (((1,), (1,)), ((), ())), preferred_element_type=F32)
    sub = lax.broadcasted_iota(jnp.int32, (heads, n), 0)
    lane = lax.broadcasted_iota(jnp.int32, (heads, n), 1)
    own = _imod(lane, heads) == sub
    s = jnp.sum(jnp.where(own, st, 0.0), axis=0, keepdims=True) * scale
    bias = jnp.concatenate([aft_ref[j] for j in range(pages)], axis=1)
    s = s + jnp.tile(fn_ref[...], (1, pages)) + bias

    m_page = s[:, 0:LANES]
    for j in range(1, nb):
        m_page = jnp.maximum(m_page, s[:, j * LANES:(j + 1) * LANES])
    m_page = _lane_class_reduce(m_page, jnp.maximum, heads)
    m_prev = m_sc[...]
    m_new = jnp.maximum(m_prev, m_page)
    alpha = jnp.exp(m_prev - m_new)
    pr = jnp.exp(s - jnp.tile(m_new, (1, nb)))
    l_page = pr[:, 0:LANES]
    for j in range(1, nb):
        l_page = l_page + pr[:, j * LANES:(j + 1) * LANES]
    l_sc[...] = alpha * l_sc[...] + l_page
    m_sc[...] = m_new
    pm = jnp.where(own, jnp.broadcast_to(pr, (heads, n)), 0.0).astype(BF16)
    acc_sc[...] = acc_sc[...] * _lane_to_col(alpha, heads) + _dot(pm, v2)

    @pl.when(p == pl.num_programs(1) - 1)
    def _():
        s_new = jnp.sum(q.astype(F32) * kn_ref[...], axis=1, keepdims=True) * scale
        m_col = _lane_to_col(m_sc[...], heads)
        l_col = _lane_to_col(_lane_class_reduce(l_sc[...], jnp.add, heads), heads)
        m_fin = jnp.maximum(m_col, s_new)
        a_col = jnp.exp(m_col - m_fin)
        p_new = jnp.exp(s_new - m_fin)
        num = acc_sc[...] * a_col + p_new * vn_ref[...]
        o_ref[...] = (num / (l_col * a_col + p_new)).astype(o_ref.dtype)


def fox_sample(page_table, q, kn, vn, fn_row, after_row, cache_k, cache_v, layer):
    db, npg = page_table.shape
    _, _, ps, heads, dh = cache_k.shape
    n = ps * heads
    assert LANES % heads == 0 and n % LANES == 0
    pages = math.gcd(npg, ATTN_PAGES_PER_STEP)
    tok = pl.BlockSpec((None, heads, dh), lambda b, p, pt: (b, 0, 0))
    page = lambda s: pl.BlockSpec((None, None, ps, heads, dh),
                                  lambda b, p, pt: (layer, pt[b, p * pages + s], 0, 0, 0))
    grid_spec = pltpu.PrefetchScalarGridSpec(
        num_scalar_prefetch=1,
        grid=(db, npg // pages),
        in_specs=[tok, tok, tok,
                  pl.BlockSpec((None, 1, n), lambda b, p, pt: (b, 0, 0)),
                  pl.BlockSpec((None, pages, 1, n), lambda b, p, pt: (b, p, 0, 0))]
                 + [page(s) for s in range(pages)] * 2,
        out_specs=tok,
        scratch_shapes=[pltpu.VMEM((1, LANES), F32), pltpu.VMEM((1, LANES), F32),
                        pltpu.VMEM((heads, dh), F32)],
    )
    return pl.pallas_call(
        functools.partial(_fox_sample_kernel, scale=dh ** -0.5, heads=heads, pages=pages),
        grid_spec=grid_spec,
        out_shape=jax.ShapeDtypeStruct((db, heads, dh), BF16),
        compiler_params=_params("parallel", "arbitrary"),
    )(page_table, q, kn, vn, fn_row, after_row, *([cache_k] * pages), *([cache_v] * pages))


def _s5_discretise(a_re, a_im, log_dt):
    dt = jnp.exp(log_dt)
    mag = jnp.exp(a_re * dt)
    ang = a_im * dt
    abar_re = mag * jnp.cos(ang)
    abar_im = mag * jnp.sin(ang)
    den = a_re * a_re + a_im * a_im
    num_re = abar_re - 1.0
    zoh_re = (num_re * a_re + abar_im * a_im) / den
    zoh_im = (abar_im * a_re - num_re * a_im) / den
    return abar_re, abar_im, zoh_re, zoh_im


def _cmul(ar, ai, br, bi):
    return ar * br - ai * bi, ar * bi + ai * br


def _s5_tables_kernel(ar_r, ai_r, ld_r, br_r, bi_r, ar_t, ai_t, ld_t, cr_t, ci_t, ar_f, ai_f, ld_f,
                      kt_ref, wt_ref, et_ref, w0_ref, e0_ref, ap_ref, *, chunk, gsz, psz):
    ngrp = LANES // gsz
    pw = ngrp * psz
    abr, abi, zr, zi = _s5_discretise(ar_r[...], ai_r[...], ld_r[...])
    bbr, bbi = _cmul(zr, zi, br_r[...], bi_r[...])
    atr, ati, _, _ = _s5_discretise(ar_t[...], ai_t[...], ld_t[...])
    ctr, cti = cr_t[...], ci_t[...]
    afr, afi, _, _ = _s5_discretise(ar_f[...], ai_f[...], ld_f[...])

    def group_of(shape, axis, size):
        return _idiv(lax.broadcasted_iota(jnp.int32, shape, axis), size)

    same_kk = group_of((LANES, LANES), 0, gsz) == group_of((LANES, LANES), 1, gsz)
    same_wp = group_of((LANES, pw), 0, gsz) == group_of((LANES, pw), 1, psz)
    same_pe = group_of((pw, LANES), 0, psz) == group_of((pw, LANES), 1, gsz)
    rep = jnp.where(_imod(lax.broadcasted_iota(jnp.int32, (psz, pw), 1), psz)
                    == lax.broadcasted_iota(jnp.int32, (psz, pw), 0), 1.0, 0.0).astype(F32)

    def w_table(wr, wi):
        return jnp.concatenate([jnp.where(same_wp, _dot_f32(wr, rep), 0.0),
                                jnp.where(same_wp, _dot_f32(wi, rep), 0.0)], axis=1)

    def e_table(qr, qi):
        return jnp.concatenate([jnp.where(same_pe, jnp.tile(qr, (ngrp, 1)), 0.0),
                                jnp.where(same_pe, jnp.tile(-qi, (ngrp, 1)), 0.0)], axis=0)

    w0_ref[...] = w_table(bbr, bbi)
    e0_ref[...] = e_table(ctr, cti)
    ap_ref[0:1, :] = jnp.concatenate([afr, afi], axis=1)

    pr_r, pi_r = jnp.ones_like(abr), jnp.zeros_like(abr)
    pr_t, pi_t = jnp.ones_like(atr), jnp.zeros_like(atr)
    pr_f, pi_f = jnp.ones_like(afr), jnp.zeros_like(afr)
    for tau in range(chunk):
        wr, wi = _cmul(pr_r, pi_r, bbr, bbi)
        kt_ref[tau] = jnp.where(same_kk, _dot_f32(wr, ctr) - _dot_f32(wi, cti), 0.0).astype(kt_ref.dtype)
        wt_ref[tau] = w_table(wr, wi).astype(wt_ref.dtype)
        pr_r, pi_r = _cmul(pr_r, pi_r, abr, abi)
        pr_t, pi_t = _cmul(pr_t, pi_t, atr, ati)
        pr_f, pi_f = _cmul(pr_f, pi_f, afr, afi)
        qr, qi = _cmul(ctr, cti, pr_t, pi_t)
        et_ref[tau] = e_table(qr, qi).astype(et_ref.dtype)
    ap_ref[1:2, :] = jnp.concatenate([pr_f, pi_f], axis=1)


def s5_tables(a_re, a_im, log_dt, b_re, b_im, c_re, c_im):
    g, p = a_re.shape
    c = b_re.shape[2]
    assert LANES % c == 0 and (g * c) % LANES == 0
    nl = g * c // LANES
    ngrp = LANES // c
    pw = ngrp * p
    chunk = S5_CHUNK
    ld = jnp.broadcast_to(log_dt[:, None], (g, p))
    rows = lambda a: jnp.repeat(a, c, axis=0)
    lanes = lambda a: jnp.repeat(a.T, c, axis=1)
    flat = lambda a: a.reshape(1, g * p)
    args = [rows(a_re), rows(a_im), rows(ld),
            b_re.transpose(0, 2, 1).reshape(g * c, p), b_im.transpose(0, 2, 1).reshape(g * c, p),
            lanes(a_re), lanes(a_im), lanes(ld),
            c_re.transpose(2, 0, 1).reshape(p, g * c), c_im.transpose(2, 0, 1).reshape(p, g * c),
            flat(a_re), flat(a_im), flat(ld)]
    spec_r = pl.BlockSpec((LANES, p), lambda i: (i, 0))
    spec_t = pl.BlockSpec((p, LANES), lambda i: (0, i))
    spec_f = pl.BlockSpec((1, pw), lambda i: (0, i))
    return pl.pallas_call(
        functools.partial(_s5_tables_kernel, chunk=chunk, gsz=c, psz=p),
        grid=(nl,),
        in_specs=[spec_r] * 5 + [spec_t] * 5 + [spec_f] * 3,
        out_specs=[pl.BlockSpec((None, chunk, LANES, LANES), lambda i: (i, 0, 0, 0)),
                   pl.BlockSpec((None, chunk, LANES, 2 * pw), lambda i: (i, 0, 0, 0)),
                   pl.BlockSpec((None, chunk, 2 * pw, LANES), lambda i: (i, 0, 0, 0)),
                   pl.BlockSpec((None, LANES, 2 * pw), lambda i: (i, 0, 0)),
                   pl.BlockSpec((None, 2 * pw, LANES), lambda i: (i, 0, 0)),
                   pl.BlockSpec((None, 2, 2 * pw), lambda i: (i, 0, 0))],
        out_shape=[jax.ShapeDtypeStruct((nl, chunk, LANES, LANES), BF16),
                   jax.ShapeDtypeStruct((nl, chunk, LANES, 2 * pw), BF16),
                   jax.ShapeDtypeStruct((nl, chunk, 2 * pw, LANES), BF16),
                   jax.ShapeDtypeStruct((nl, LANES, 2 * pw), F32),
                   jax.ShapeDtypeStruct((nl, 2 * pw, LANES), F32),
                   jax.ShapeDtypeStruct((nl, 2, 2 * pw), F32)],
        compiler_params=_params("parallel"),
    )(*args)


def _s5_prompt_kernel(u_ref, d_ref, kt_ref, wt_ref, et_ref, ap_ref, z_ref, hre_ref, him_ref, s_sc,
                      *, chunk, bsz):
    m = u_ref.shape[0]
    rows = m // chunk
    nchunk = rows // bsz
    pw = ap_ref.shape[1] // 2
    u = [u_ref[pl.ds(i, rows, stride=chunk), :] for i in range(chunk)]
    ub = [x.astype(BF16) for x in u]
    s = _dot(ub[0], wt_ref[chunk - 1])
    for i in range(1, chunk):
        s = s + _dot(ub[i], wt_ref[chunk - 1 - i])
    s_sc[...] = s
    a_r = ap_ref[1:2, 0:pw]
    a_i = ap_ref[1:2, pw:2 * pw]
    for b in range(bsz):
        def body(g8, carry):
            hr, hi = carry
            base = pl.multiple_of(b * nchunk + g8 * SUBLANES, SUBLANES)
            tile = s_sc[pl.ds(base, SUBLANES), :]
            before_r, before_i = [], []
            for r in range(SUBLANES):
                before_r.append(hr)
                before_i.append(hi)
                nr = a_r * hr - a_i * hi + tile[r:r + 1, 0:pw]
                ni = a_r * hi + a_i * hr + tile[r:r + 1, pw:2 * pw]
                hr, hi = nr, ni
            s_sc[pl.ds(base, SUBLANES), 0:pw] = jnp.concatenate(before_r, axis=0)
            s_sc[pl.ds(base, SUBLANES), pw:2 * pw] = jnp.concatenate(before_i, axis=0)
            return hr, hi

        zero = jnp.zeros((1, pw), F32)
        hr, hi = lax.fori_loop(0, nchunk // SUBLANES, body, (zero, zero))
        hre_ref[b:b + 1, :] = hr
        him_ref[b:b + 1, :] = hi
    h_in = s_sc[...].astype(BF16)
    d = d_ref[...]
    for i in range(chunk):
        y = _dot(h_in, et_ref[i])
        for j in range(i + 1):
            y = y + _dot(ub[j], kt_ref[i - j])
        y = y + d * u[i]
        z_ref[pl.ds(i, rows, stride=chunk), :] = _gelu_tanh(y)


def s5_prompt(u, d_skip, kt, wt, et, ap, bsz):
    m, width = u.shape
    nl = width // LANES
    chunk = kt.shape[1]
    pw2 = ap.shape[2]
    rows = m // chunk
    assert (m // bsz) % (chunk * SUBLANES) == 0
    col = pl.BlockSpec((m, LANES), lambda i: (0, i))
    tab = lambda a: pl.BlockSpec((None,) + a.shape[1:], lambda i: (i,) + (0,) * (a.ndim - 1))
    st = pl.BlockSpec((bsz, pw2 // 2), lambda i: (0, i))
    return pl.pallas_call(
        functools.partial(_s5_prompt_kernel, chunk=chunk, bsz=bsz),
        grid=(nl,),
        in_specs=[col, pl.BlockSpec((1, LANES), lambda i: (0, i)), tab(kt), tab(wt), tab(et), tab(ap)],
        out_specs=[col, st, st],
        out_shape=[jax.ShapeDtypeStruct((m, width), F32),
                   jax.ShapeDtypeStruct((bsz, nl * pw2 // 2), F32),
                   jax.ShapeDtypeStruct((bsz, nl * pw2 // 2), F32)],
        scratch_shapes=[pltpu.VMEM((rows, pw2), F32)],
        compiler_params=_params("parallel"),
    )(u, d_skip.reshape(1, width), kt, wt, et, ap)


def _s5_sample_kernel(u_ref, d_ref, hre_ref, him_ref, w0_ref, e0_ref, ap_ref, z_ref, ore_ref, oim_ref):
    pw = hre_ref.shape[1]
    u = u_ref[...]
    bu = _dot_f32(u, w0_ref[...])
    a_r = ap_ref[0:1, 0:pw]
    a_i = ap_ref[0:1, pw:2 * pw]
    hr, hi = hre_ref[...], him_ref[...]
    nr = a_r * hr - a_i * hi + bu[:, 0:pw]
    ni = a_r * hi + a_i * hr + bu[:, pw:2 * pw]
    y = _dot_f32(jnp.concatenate([nr, ni], axis=1), e0_ref[...]) + d_ref[...] * u
    z_ref[...] = _gelu_tanh(y)
    ore_ref[...] = nr
    oim_ref[...] = ni


def s5_sample(u, d_skip, h_re, h_im, w0, e0, ap):
    m, width = u.shape
    nl = width // LANES
    pw2 = ap.shape[2]
    col = pl.BlockSpec((m, LANES), lambda i: (0, i))
    tab = lambda a: pl.BlockSpec((None,) + a.shape[1:], lambda i: (i,) + (0,) * (a.ndim - 1))
    st = pl.BlockSpec((m, pw2 // 2), lambda i: (0, i))
    return pl.pallas_call(
        _s5_sample_kernel,
        grid=(nl,),
        in_specs=[col, pl.BlockSpec((1, LANES), lambda i: (0, i)), st, st, tab(w0), tab(e0), tab(ap)],
        out_specs=[col, st, st],
        out_shape=[jax.ShapeDtypeStruct((m, width), F32),
                   jax.ShapeDtypeStruct(h_re.shape, F32),
                   jax.ShapeDtypeStruct(h_im.shape, F32)],
        compiler_params=_params("parallel"),
    )(u, d_skip.reshape(1, width), h_re, h_im, w0, e0, ap)


def _layer(x, l, big_w, lw, kv, dims, prompt, sample_state):
    cw, fw, sw, heads, dh, d_ff = dims
    m, d = x.shape
    tm = min(1024, m)
    xn = rmsnorm(x, lw["norm_mix"], BF16)
    th = min(512, cw)
    head = lambda c0, n, dt, stack=None: matmul_ws(xn, [(big_w["w_in"], c0 // th)], n, _ep_plain, [], dt,
                                                   tm, th, layer=l, stack=stack)
    hbc = head(0, 3 * cw, F32)
    q = head(3 * cw, fw, BF16)
    k_all = head(3 * cw + fw, fw, F32, stack=(kv[0], l))
    v_all = head(3 * cw + 2 * fw, fw, F32, stack=(kv[1], l))
    tt = min(1024, sw)
    u_c = matmul_ws(xn, [(lw["w_tail"], 0)], sw, _ep_plain, [], F32, tm, tt)
    sg = matmul_ws(xn, [(lw["w_tail"], sw // tt)], 3 * d, _ep_sigmoid, [], F32, tm, tt)
    logf = matmul_ws(xn, [(lw["w_f"], 0)], LANES, _ep_logf, [(lw["fox_bias"], "row", 0)], F32, tm, LANES)

    kt, wt, et, w0, e0, ap = lw["s5_tables"]
    if prompt is not None:
        bsz, t = prompt
        y_a, conv_new = conv_prompt(hbc, lw["conv_wt"], bsz, t)
        f, ft = forget_cumsum(logf, bsz, t)
        y_b = fox_prompt(q, k_all, v_all, l, f, ft, bsz, t, heads, dh)
        z, h_re, h_im = s5_prompt(u_c, lw["ssm_d"], kt, wt, et, ap, bsz)
    else:
        st = sample_state
        buf = st["conv"]
        y_a, u_a = conv_sample(hbc, buf[:, 0], buf[:, 1], lw["conv_wt"])
        conv_new = jnp.stack([buf[:, 1], u_a], axis=1)
        lf_new = logf[:, :heads]
        ps = st["cache_k"].shape[2]
        after = forget_after(st["page_table"], st["cache_logf"], l)
        db, npg = st["page_table"].shape
        y_b = fox_sample(st["page_table"], q.reshape(m, heads, dh), k_all[l].reshape(m, heads, dh),
                         v_all[l].reshape(m, heads, dh), jnp.tile(lf_new, (1, ps)).reshape(db, 1, ps * heads),
                         after.reshape(db, npg, 1, ps * heads), st["cache_k"], st["cache_v"], l)
        y_b = y_b.reshape(m, fw)
        z, h_re, h_im = s5_sample(u_c, lw["ssm_d"], st["h_re"], st["h_im"], w0, e0, ap)
    y_c = matmul(z, [(lw["w_glu"], 0)], sw, _ep_glu, [(z, "tile", 0)], BF16, tm, 1024)

    merged = gated_merge(y_a, y_b, y_c, big_w["w_branch"], l, sg, 512, 512)
    x = matmul_ws(merged, [(big_w["w_out"], 0)], d, _ep_residual, [(x, "tile", 0)], F32, tm, 512, layer=l)
    xn2 = rmsnorm(x, lw["norm_ffn"], BF16)
    tf = 256
    hid = matmul_ws(xn2, [(big_w["w_up"], 0), (big_w["w_up"], d_ff // tf)], d_ff, _ep_swiglu, [], BF16,
                    tm, tf, layer=l)
    x = matmul(hid, [(lw["w_down"], 0)], d, _ep_residual, [(x, "tile", 0)], F32, 512, 512)
    return x, (k_all, v_all), logf, conv_new, h_re, h_im


def kernel(x_prompt, x_sample, cache_k, cache_v, cache_logf, state_conv, state_ssm_re, state_ssm_im,
           page_table, norm_mix, w_in, conv_w, fox_bias, ssm_a_re, ssm_a_im, ssm_log_dt, ssm_b_re,
           ssm_b_im, ssm_c_re, ssm_c_im, ssm_d, w_glu, w_branch, w_out, norm_ffn, w_up, w_down,
           norm_final):
    bp, t, d = x_prompt.shape
    bd, td, _ = x_sample.shape
    depth = w_in.shape[0]
    cw = conv_w.shape[1]
    heads = fox_bias.shape[1]
    dh = cache_k.shape[4]
    fw = heads * dh
    g, p = ssm_a_re.shape[1:]
    sw = ssm_d.shape[1]
    d_ff = w_down.shape[1]
    assert td == 1
    dims = (cw, fw, sw, heads, dh, d_ff)
    f0 = 3 * cw + 3 * fw

    big_w = {"w_in": w_in, "w_branch": w_branch, "w_out": w_out, "w_up": w_up}
    xp = x_prompt.reshape(bp * t, d)
    xs = x_sample.reshape(bd * td, d)
    kv_p = (jnp.zeros((depth, bp * t, fw), F32), jnp.zeros((depth, bp * t, fw), F32))
    kv_s = (jnp.zeros((depth, bd * td, fw), F32), jnp.zeros((depth, bd * td, fw), F32))
    outs_p, outs_s = [], []
    for l in range(depth):
        lw = {
            "norm_mix": norm_mix[l], "norm_ffn": norm_ffn[l],
            "w_tail": w_in[l, :, f0 + heads:].astype(BF16),
            "w_f": jnp.pad(w_in[l, :, f0:f0 + heads], ((0, 0), (0, LANES - heads))).astype(BF16),
            "fox_bias": jnp.pad(fox_bias[l], (0, LANES - heads)).reshape(1, LANES),
            "conv_wt": conv_w[l].T,
            "ssm_d": ssm_d[l],
            "w_glu": w_glu[l].astype(BF16),
            "w_down": w_down[l].astype(BF16),
            "s5_tables": s5_tables(ssm_a_re[l], ssm_a_im[l], ssm_log_dt[l], ssm_b_re[l], ssm_b_im[l],
                                   ssm_c_re[l], ssm_c_im[l]),
        }
        xp, kv_p, lfp, cvp, hrp, hip = _layer(xp, l, big_w, lw, kv_p, dims, (bp, t), None)
        outs_p.append((lfp[:, :heads].reshape(bp, t, heads), cvp, hrp.reshape(bp, g, p), hip.reshape(bp, g, p)))
        st = {"conv": state_conv[l], "h_re": state_ssm_re[l].reshape(bd, g * p),
              "h_im": state_ssm_im[l].reshape(bd, g * p), "page_table": page_table,
              "cache_k": cache_k, "cache_v": cache_v, "cache_logf": cache_logf}
        xs, kv_s, lfs, cvs, hrs, his = _layer(xs, l, big_w, lw, kv_s, dims, None, st)
        outs_s.append((lfs[:, :heads].reshape(bd, td, heads), cvs, hrs.reshape(bd, g, p), his.reshape(bd, g, p)))
    stack = lambda outs: [jnp.stack(a, axis=0) for a in zip(*outs)]
    y_prompt = rmsnorm(xp, norm_final, F32).reshape(bp, t, d)
    y_sample = rmsnorm(xs, norm_final, F32).reshape(bd, td, d)
    kv5 = lambda a, b, n: a.reshape(depth, b, n, heads, dh)
    return (y_prompt, y_sample, kv5(kv_p[0], bp, t), kv5(kv_p[1], bp, t), *stack(outs_p),
            kv5(kv_s[0], bd, td), kv5(kv_s[1], bd, td), *stack(outs_s))
```

```python
import functools
import math

import jax
import jax.numpy as jnp
from jax import lax
from jax.experimental import pallas as pl
from jax.experimental.pallas import tpu as pltpu

F32 = jnp.float32
BF16 = jnp.bfloat16
HIGHEST = lax.Precision.HIGHEST

RMS_EPS = 1e-6
LOG2E = math.log2(math.e)
LANES = 128
SUBLANES = 8
VMEM_LIMIT = 56 * 1024 * 1024
S5_CHUNK = 8
NEG_BIG = -1e30
FOX_ROW_CHUNK = 32
ATTN_PAGES_PER_STEP = 4
AFTER_PAGES_PER_STEP = 16


def _params(*sem):
    return pltpu.CompilerParams(dimension_semantics=sem, vmem_limit_bytes=VMEM_LIMIT)


def _sigmoid(x):
    return 1.0 / (1.0 + jnp.exp(-x))


def _log_sigmoid(x):
    t = -x
    return -(jnp.maximum(t, 0.0) + jnp.log1p(jnp.exp(-jnp.abs(t))))


def _gelu_tanh(x):
    c = math.sqrt(2.0 / math.pi)
    return x * (0.5 * (1.0 + jnp.tanh(c * (x + 0.044715 * (x * x * x)))))


def _idiv(x, n):
    assert n & (n - 1) == 0
    return x >> (n.bit_length() - 1)


def _imod(x, n):
    assert n & (n - 1) == 0
    return x & (n - 1)


def _dot(a, b):
    return jnp.dot(a, b, preferred_element_type=F32)


def _dot_f32(a, b):
    return jnp.dot(a, b, preferred_element_type=F32, precision=HIGHEST)


def _rmsnorm_kernel(x_ref, g_ref, o_ref):
    x = x_ref[...]
    ms = jnp.mean(x * x, axis=-1, keepdims=True)
    o_ref[...] = (x * lax.rsqrt(ms + RMS_EPS) * g_ref[...]).astype(o_ref.dtype)


def rmsnorm(x, g, out_dtype):
    m, d = x.shape
    tm = min(256, m)
    return pl.pallas_call(
        _rmsnorm_kernel,
        grid=(m // tm,),
        in_specs=[pl.BlockSpec((tm, d), lambda i: (i, 0)),
                  pl.BlockSpec((1, d), lambda i: (0, 0))],
        out_specs=pl.BlockSpec((tm, d), lambda i: (i, 0)),
        out_shape=jax.ShapeDtypeStruct((m, d), out_dtype),
        compiler_params=_params("parallel"),
    )(x, g.reshape(1, d))


def _mm_kernel(*refs, n_w, epilogue):
    x_ref = refs[0]
    w_refs = refs[1:1 + n_w]
    e_refs = refs[1 + n_w:-1]
    o_ref = refs[-1]
    x = x_ref[...]
    if x.dtype != BF16:
        x = x.astype(BF16)
    accs = [_dot(x, w[...]) for w in w_refs]
    o_ref[...] = epilogue(accs, [e[...] for e in e_refs]).astype(o_ref.dtype)


def matmul(x, ws, n, epilogue, extras, out_dtype, tm, tn, layer=None):
    m, k = x.shape
    tm = min(tm, m)
    tn = min(tn, n)
    assert m % tm == 0 and n % tn == 0
    in_specs = [pl.BlockSpec((tm, k), lambda i, j: (i, 0))]
    args = [x]
    for w, c0 in ws:
        if w.ndim == 3:
            in_specs.append(pl.BlockSpec((None, k, tn), lambda i, j, c0=c0: (layer, 0, c0 + j)))
        else:
            in_specs.append(pl.BlockSpec((k, tn), lambda i, j, c0=c0: (0, c0 + j)))
        args.append(w)
    for a, kind, c0 in extras:
        if kind == "tile":
            in_specs.append(pl.BlockSpec((tm, tn), lambda i, j, c0=c0: (i, c0 + j)))
        else:
            in_specs.append(pl.BlockSpec((1, tn), lambda i, j, c0=c0: (0, c0 + j)))
        args.append(a)
    return pl.pallas_call(
        functools.partial(_mm_kernel, n_w=len(ws), epilogue=epilogue),
        grid=(m // tm, n // tn),
        in_specs=in_specs,
        out_specs=pl.BlockSpec((tm, tn), lambda i, j: (i, j)),
        out_shape=jax.ShapeDtypeStruct((m, n), out_dtype),
        compiler_params=_params("parallel", "parallel"),
    )(*args)


def _mmw_kernel(*refs, n_w, n_e, transposed, epilogue):
    x_ref = refs[0]
    w_refs = refs[1:1 + n_w]
    e_refs = refs[1 + n_w:1 + n_w + n_e]
    o_ref = refs[-1 - n_w]
    wb_refs = refs[-n_w:]

    @pl.when(pl.program_id(1) == 0)
    def _():
        for w, wb in zip(w_refs, wb_refs):
            wb[...] = (w[0] if transposed else w[...]).astype(BF16)

    x = x_ref[...]
    if x.dtype != BF16:
        x = x.astype(BF16)
    if transposed:
        accs = [lax.dot_general(x, wb[...], (((1,), (1,)), ((), ())), preferred_element_type=F32)
                for wb in wb_refs]
    else:
        accs = [_dot(x, wb[...]) for wb in wb_refs]
    o_ref[...] = epilogue(accs, [e[...] for e in e_refs]).astype(o_ref.dtype)


def matmul_ws(x, ws, n, epilogue, extras, out_dtype, tm, tn, layer, transposed=False, stack=None):
    m, k = x.shape
    tm = min(tm, m)
    tn = min(tn, n)
    assert m % tm == 0 and n % tn == 0
    in_specs = [pl.BlockSpec((tm, k), lambda j, i: (i, 0))]
    args = [x]
    for w, off in ws:
        if transposed:
            in_specs.append(pl.BlockSpec((pl.Element(1), pl.Element(tn), pl.Element(k)),
                                         lambda j, i, off=off: (layer, pl.multiple_of(off + j * tn, SUBLANES), 0)))
        else:
            in_specs.append(pl.BlockSpec((None, k, tn), lambda j, i, off=off: (layer, 0, off + j)))
        args.append(w)
    for a, kind, c0 in extras:
        if kind == "tile":
            in_specs.append(pl.BlockSpec((tm, tn), lambda j, i, c0=c0: (i, c0 + j)))
        else:
            in_specs.append(pl.BlockSpec((1, tn), lambda j, i, c0=c0: (0, c0 + j)))
        args.append(a)
    aliases = {}
    if stack is None:
        out_spec = pl.BlockSpec((tm, tn), lambda j, i: (i, j))
        out_shape = jax.ShapeDtypeStruct((m, n), out_dtype)
    else:
        buf, slab, shape = stack
        out_spec = pl.BlockSpec((None, tm, tn), lambda j, i: (slab, i, j))
        out_shape = jax.ShapeDtypeStruct(shape, out_dtype)
        if buf is not None:
            in_specs.append(pl.BlockSpec(memory_space=pl.ANY))
            args.append(buf)
            aliases = {len(args) - 1: 0}
    return pl.pallas_call(
        functools.partial(_mmw_kernel, n_w=len(ws), n_e=len(extras), transposed=transposed,
                          epilogue=epilogue),
        grid=(n // tn, m // tm),
        in_specs=in_specs,
        out_specs=out_spec,
        out_shape=out_shape,
        scratch_shapes=[pltpu.VMEM((tn, k) if transposed else (k, tn), BF16) for _ in ws],
        input_output_aliases=aliases,
        compiler_params=_params("parallel", "arbitrary"),
    )(*args)


def _ep_plain(accs, extras):
    return accs[0]


def _ep_sigmoid(accs, extras):
    return _sigmoid(accs[0])


def _ep_logf(accs, extras):
    return _log_sigmoid(accs[0] + extras[0])


def _ep_residual(accs, extras):
    return extras[0] + accs[0]


def _ep_glu(accs, extras):
    return extras[0] * _sigmoid(accs[0])


def _ep_swiglu(accs, extras):
    g = accs[0]
    return (g * _sigmoid(g)) * accs[1]


def _merge_kernel(ya_ref, yb_ref, yc_ref, w_ref, ga_ref, gb_ref, gc_ref, o_ref, wb_ref):
    @pl.when(pl.program_id(1) == 0)
    def _():
        wb_ref[...] = w_ref[...].astype(BF16)

    ka = ya_ref.shape[1]
    kb = ka + yb_ref.shape[1]
    a = _dot(ya_ref[...], wb_ref[0:ka, :])
    b = _dot(yb_ref[...], wb_ref[ka:kb, :])
    c = _dot(yc_ref[...], wb_ref[kb:, :])
    o_ref[...] = (ga_ref[...] * a + gb_ref[...] * b + gc_ref[...] * c).astype(o_ref.dtype)


def gated_merge(ya, yb, yc, w_branch, layer, sg, tm, tn):
    m = ya.shape[0]
    _, mix, d = w_branch.shape
    tm = min(tm, m)
    tn = min(tn, d)
    nj = d // tn
    row = lambda a: pl.BlockSpec((tm, a.shape[1]), lambda j, i: (i, 0))
    gate = lambda c: pl.BlockSpec((tm, tn), lambda j, i, c=c: (i, c * nj + j))
    return pl.pallas_call(
        _merge_kernel,
        grid=(nj, m // tm),
        in_specs=[row(ya), row(yb), row(yc),
                  pl.BlockSpec((None, mix, tn), lambda j, i: (layer, 0, j)),
                  gate(0), gate(1), gate(2)],
        out_specs=pl.BlockSpec((tm, tn), lambda j, i: (i, j)),
        out_shape=jax.ShapeDtypeStruct((m, d), BF16),
        scratch_shapes=[pltpu.VMEM((mix, tn), BF16)],
        compiler_params=_params("parallel", "arbitrary"),
    )(ya, yb, yc, w_branch, sg, sg, sg)


def _conv_prompt_kernel(h_ref, b_ref, c_ref, w_ref, y_ref, st_ref):
    u = c_ref[...] * h_ref[...]
    t = u.shape[0]
    row = lax.broadcasted_iota(jnp.int32, u.shape, 0)
    u1 = jnp.where(row >= 1, pltpu.roll(u, 1, 0), 0.0)
    u2 = jnp.where(row >= 2, pltpu.roll(u, 2, 0), 0.0)
    w = w_ref[...]
    conv = u2 * w[0:1, :] + u1 * w[1:2, :] + u * w[2:3, :]
    y_ref[...] = (b_ref[...] * conv).astype(y_ref.dtype)
    st_ref[...] = u[t - 2:t, :]


def conv_prompt(hbc, wt, bsz, t):
    cw = wt.shape[1]
    tc = min(256, cw)
    nc = cw // tc
    hbc3 = hbc.reshape(bsz, t, 3 * cw)
    spec = lambda off: pl.BlockSpec((None, t, tc), lambda b, c, off=off: (b, 0, off * nc + c))
    y, st = pl.pallas_call(
        _conv_prompt_kernel,
        grid=(bsz, nc),
        in_specs=[spec(0), spec(1), spec(2), pl.BlockSpec((3, tc), lambda b, c: (0, c))],
        out_specs=[pl.BlockSpec((None, t, tc), lambda b, c: (b, 0, c)),
                   pl.BlockSpec((None, 2, tc), lambda b, c: (b, 0, c))],
        out_shape=[jax.ShapeDtypeStruct((bsz, t, cw), BF16),
                   jax.ShapeDtypeStruct((bsz, 2, cw), F32)],
        compiler_params=_params("parallel", "parallel"),
    )(hbc3, hbc3, hbc3, wt)
    return y.reshape(bsz * t, cw), st


def _conv_sample_kernel(h_ref, b_ref, c_ref, buf0_ref, buf1_ref, w_ref, y_ref, u_ref):
    u = c_ref[...] * h_ref[...]
    w = w_ref[...]
    conv = buf0_ref[...] * w[0:1, :] + buf1_ref[...] * w[1:2, :] + u * w[2:3, :]
    y_ref[...] = (b_ref[...] * conv).astype(y_ref.dtype)
    u_ref[...] = u


def conv_sample(hbc, buf0, buf1, wt):
    m, cw = buf0.shape
    spec = lambda off: pl.BlockSpec((m, cw), lambda i, off=off: (0, off))
    return pl.pallas_call(
        _conv_sample_kernel,
        grid=(1,),
        in_specs=[spec(0), spec(1), spec(2), spec(0), spec(0), pl.BlockSpec((3, cw), lambda i: (0, 0))],
        out_specs=[spec(0), spec(0)],
        out_shape=[jax.ShapeDtypeStruct((m, cw), BF16), jax.ShapeDtypeStruct((m, cw), F32)],
        compiler_params=_params("arbitrary"),
    )(hbc, hbc, hbc, buf0, buf1, wt)


def _fcum_kernel(lf_ref, f_ref, ft_ref):
    x = lf_ref[...]
    t = x.shape[0]
    row = lax.broadcasted_iota(jnp.int32, x.shape, 0)
    s = 1
    while s < t:
        x = x + jnp.where(row >= s, pltpu.roll(x, s, 0), 0.0)
        s *= 2
    f_ref[...] = x
    ft_ref[...] = x.T


def forget_cumsum(logf, bsz, t):
    lf3 = logf.reshape(bsz, t, LANES)
    return pl.pallas_call(
        _fcum_kernel,
        grid=(bsz,),
        in_specs=[pl.BlockSpec((None, t, LANES), lambda b: (b, 0, 0))],
        out_specs=[pl.BlockSpec((None, t, LANES), lambda b: (b, 0, 0)),
                   pl.BlockSpec((None, LANES, t), lambda b: (b, 0, 0))],
        out_shape=[jax.ShapeDtypeStruct((bsz, t, LANES), F32),
                   jax.ShapeDtypeStruct((bsz, LANES, t), F32)],
        compiler_params=_params("parallel"),
    )(lf3)


def _fox_prompt_kernel(q_ref, k_ref, v_ref, f_ref, ft_ref, o_ref, m_sc, acc_sc, fq_sc, s_sc, p_sc, pm_sc,
                       *, heads, dh, scale, tq, tk):
    qi = pl.program_id(1)
    ki = pl.program_id(2)
    rc = FOX_ROW_CHUNK
    nlb = tk // LANES

    @pl.when(ki == 0)
    def _():
        m_sc[...] = jnp.full(m_sc.shape, NEG_BIG, F32)
        acc_sc[...] = jnp.zeros(acc_sc.shape, F32)
        fq = f_ref[...] * LOG2E
        for h in range(heads):
            fq_sc[h] = jnp.broadcast_to(fq[:, h:h + 1], (tq, LANES))

    def block(diagonal):
        fk = ft_ref[...] * LOG2E
        ones = jnp.ones((tk, LANES), BF16)
        ahead = (lax.broadcasted_iota(jnp.int32, (rc, tk), 1) - lax.broadcasted_iota(jnp.int32, (rc, tk), 0))
        for h in range(heads):
            sl = slice(h * dh, (h + 1) * dh)
            buf = h % 2
            kh = k_ref[:, sl].astype(BF16)
            vh = jnp.concatenate([v_ref[:, sl].astype(BF16), ones], axis=1)
            s_sc[buf] = lax.dot_general(q_ref[:, sl], kh, (((1,), (1,)), ((), ())),
                                        preferred_element_type=F32)
            fk_h = fk[h:h + 1, :]
            for r0 in range(0, tq, rc):
                rows = slice(r0, r0 + rc)
                x = s_sc[buf, rows, :] * (scale * LOG2E) + jnp.tile(fq_sc[h, rows, :], (1, nlb)) - fk_h
                if diagonal:
                    x = jnp.where(ahead <= r0, x, NEG_BIG)
                s_sc[buf, rows, :] = x
                pm = x[:, 0:LANES]
                for j in range(1, nlb):
                    pm = jnp.maximum(pm, x[:, j * LANES:(j + 1) * LANES])
                pm_sc[buf, rows, :] = pm
            m_prev = m_sc[h]
            m_new = jnp.maximum(m_prev, jnp.max(pm_sc[buf], axis=1, keepdims=True))
            m_sc[h] = m_new
            pm_sc[buf] = m_new
            alpha = jnp.exp2(m_prev - m_new)
            for r0 in range(0, tq, rc):
                rows = slice(r0, r0 + rc)
                x = s_sc[buf, rows, :] - jnp.tile(pm_sc[buf, rows, :], (1, nlb))
                p_sc[buf, rows, :] = jnp.exp2(x).astype(BF16)
            acc_sc[h] = jnp.tile(alpha, (1, 2)) * acc_sc[h] + _dot(p_sc[buf], vh)

    @pl.when(ki < qi)
    def _():
        block(False)

    @pl.when(ki == qi)
    def _():
        block(True)
        for h in range(heads):
            acc = acc_sc[h]
            o_ref[:, h * dh:(h + 1) * dh] = (acc[:, 0:dh] / acc[:, dh:2 * dh]).astype(o_ref.dtype)


def fox_prompt(q, k, v, layer, f, ft, bsz, t, heads, dh):
    assert dh == LANES
    tq = tk = min(512, t)
    nq = t // tq
    width = heads * dh
    hp = min(LANES, -(-heads // SUBLANES) * SUBLANES)
    kernel = functools.partial(_fox_prompt_kernel, heads=heads, dh=dh, scale=dh ** -0.5, tq=tq, tk=tk)
    kv = pl.BlockSpec((None, tk, width), lambda b, i, j: (layer, b * nq + jnp.minimum(i, j), 0))
    return pl.pallas_call(
        kernel,
        grid=(bsz, nq, nq),
        in_specs=[
            pl.BlockSpec((tq, width), lambda b, i, j: (b * nq + i, 0)),
            kv, kv,
            pl.BlockSpec((None, tq, LANES), lambda b, i, j: (b, i, 0)),
            pl.BlockSpec((None, hp, tk), lambda b, i, j: (b, 0, jnp.minimum(i, j))),
        ],
        out_specs=pl.BlockSpec((tq, width), lambda b, i, j: (b * nq + i, 0)),
        out_shape=jax.ShapeDtypeStruct((bsz * t, width), BF16),
        scratch_shapes=[pltpu.VMEM((heads, tq, LANES), F32),
                        pltpu.VMEM((heads, tq, 2 * dh), F32),
                        pltpu.VMEM((heads, tq, LANES), F32),
                        pltpu.VMEM((2, tq, tk), F32),
                        pltpu.VMEM((2, tq, tk), BF16),
                        pltpu.VMEM((2, tq, LANES), F32)],
        compiler_params=_params("parallel", "parallel", "arbitrary"),
    )(q, k, v, f, ft)


def _after_kernel(pt_ref, *refs, pages):
    lf_refs = refs[:pages]
    o_ref, carry_sc = refs[pages], refs[pages + 1]

    @pl.when(pl.program_id(1) == 0)
    def _():
        carry_sc[...] = jnp.zeros(carry_sc.shape, F32)

    ps = lf_refs[0].shape[0]
    r = lax.broadcasted_iota(jnp.int32, (ps, ps), 0)
    c = lax.broadcasted_iota(jnp.int32, (ps, ps), 1)
    later = jnp.where(c > r, 1.0, 0.0).astype(F32)
    carry = carry_sc[...]
    for s in range(pages):
        lf = lf_refs[s][...]
        o_ref[pages - 1 - s] = _dot_f32(later, lf) + carry
        carry = carry + jnp.sum(lf, axis=0, keepdims=True)
    carry_sc[...] = carry


def forget_after(page_table, cache_logf, layer):
    db, npg = page_table.shape
    _, _, ps, heads = cache_logf.shape
    pages = math.gcd(npg, AFTER_PAGES_PER_STEP)
    nsteps = npg // pages
    lf_spec = lambda s: pl.BlockSpec((None, None, ps, heads),
                                     lambda b, p, pt: (layer, pt[b, npg - 1 - (p * pages + s)], 0, 0))
    grid_spec = pltpu.PrefetchScalarGridSpec(
        num_scalar_prefetch=1,
        grid=(db, nsteps),
        in_specs=[lf_spec(s) for s in range(pages)],
        out_specs=pl.BlockSpec((None, pages, ps, heads), lambda b, p, pt: (b, nsteps - 1 - p, 0, 0)),
        scratch_shapes=[pltpu.VMEM((1, heads), F32)],
    )
    return pl.pallas_call(
        functools.partial(_after_kernel, pages=pages),
        grid_spec=grid_spec,
        out_shape=jax.ShapeDtypeStruct((db, npg, ps, heads), F32),
        compiler_params=_params("parallel", "arbitrary"),
    )(page_table, *([cache_logf] * pages))


def _lane_class_reduce(x, op, period):
    s = period
    while s < LANES:
        x = op(x, pltpu.roll(x, s, 1))
        s *= 2
    return x


def _lane_to_col(x, heads):
    sub = lax.broadcasted_iota(jnp.int32, (heads, LANES), 0)
    lane = lax.broadcasted_iota(jnp.int32, (heads, LANES), 1)
    return jnp.sum(jnp.where(lane == sub, jnp.broadcast_to(x, (heads, LANES)), 0.0), axis=1, keepdims=True)


def _fox_sample_kernel(pt_ref, q_ref, kn_ref, vn_ref, fn_ref, aft_ref, *refs, scale, heads, pages):
    k_refs = refs[:pages]
    v_refs = refs[pages:2 * pages]
    o_ref, m_sc, l_sc, acc_sc = refs[2 * pages:]
    p = pl.program_id(1)
    ps, _, dh = k_refs[0].shape
    npg = ps * heads
    n = pages * npg
    nb = n // LANES

    @pl.when(p == 0)
    def _():
        m_sc[...] = jnp.full(m_sc.shape, NEG_BIG, F32)
        l_sc[...] = jnp.zeros(l_sc.shape, F32)
        acc_sc[...] = jnp.zeros(acc_sc.shape, F32)

    q = q_ref[...]
    k2 = jnp.concatenate([r[...].reshape(npg, dh).astype(BF16) for r in k_refs], axis=0)
    v2 = jnp.concatenate([r[...].reshape(npg, dh).astype(BF16) for r in v_refs], axis=0)
    st = lax.dot_general(q, k2, (((1,), (1,)), ((), ())), preferred_element_type=F32)
    sub = lax.broadcasted_iota(jnp.int32, (heads, n), 0)
    lane = lax.broadcasted_iota(jnp.int32, (heads, n), 1)
    own = _imod(lane, heads) == sub
    s = jnp.sum(jnp.where(own, st, 0.0), axis=0, keepdims=True) * scale
    bias = jnp.concatenate([aft_ref[j] for j in range(pages)], axis=1)
    s = s + jnp.tile(fn_ref[...], (1, pages)) + bias

    m_page = s[:, 0:LANES]
    for j in range(1, nb):
        m_page = jnp.maximum(m_page, s[:, j * LANES:(j + 1) * LANES])
    m_page = _lane_class_reduce(m_page, jnp.maximum, heads)
    m_prev = m_sc[...]
    m_new = jnp.maximum(m_prev, m_page)
    alpha = jnp.exp(m_prev - m_new)
    pr = jnp.exp(s - jnp.tile(m_new, (1, nb)))
    l_page = pr[:, 0:LANES]
    for j in range(1, nb):
        l_page = l_page + pr[:, j * LANES:(j + 1) * LANES]
    l_sc[...] = alpha * l_sc[...] + l_page
    m_sc[...] = m_new
    pm = jnp.where(own, jnp.broadcast_to(pr, (heads, n)), 0.0).astype(BF16)
    acc_sc[...] = acc_sc[...] * _lane_to_col(alpha, heads) + _dot(pm, v2)

    @pl.when(p == pl.num_programs(1) - 1)
    def _():
        s_new = jnp.sum(q.astype(F32) * kn_ref[...], axis=1, keepdims=True) * scale
        m_col = _lane_to_col(m_sc[...], heads)
        l_col = _lane_to_col(_lane_class_reduce(l_sc[...], jnp.add, heads), heads)
        m_fin = jnp.maximum(m_col, s_new)
        a_col = jnp.exp(m_col - m_fin)
        p_new = jnp.exp(s_new - m_fin)
        num = acc_sc[...] * a_col + p_new * vn_ref[...]
        o_ref[...] = (num / (l_col * a_col + p_new)).astype(o_ref.dtype)


def fox_sample(page_table, q, kn, vn, fn_row, after_row, cache_k, cache_v, layer):
    db, npg = page_table.shape
    _, _, ps, heads, dh = cache_k.shape
    n = ps * heads
    assert LANES % heads == 0 and n % LANES == 0
    pages = math.gcd(npg, ATTN_PAGES_PER_STEP)
    tok = pl.BlockSpec((None, heads, dh), lambda b, p, pt: (b, 0, 0))
    page = lambda s: pl.BlockSpec((None, None, ps, heads, dh),
                                  lambda b, p, pt: (layer, pt[b, p * pages + s], 0, 0, 0))
    grid_spec = pltpu.PrefetchScalarGridSpec(
        num_scalar_prefetch=1,
        grid=(db, npg // pages),
        in_specs=[tok, tok, tok,
                  pl.BlockSpec((None, 1, n), lambda b, p, pt: (b, 0, 0)),
                  pl.BlockSpec((None, pages, 1, n), lambda b, p, pt: (b, p, 0, 0))]
                 + [page(s) for s in range(pages)] * 2,
        out_specs=tok,
        scratch_shapes=[pltpu.VMEM((1, LANES), F32), pltpu.VMEM((1, LANES), F32),
                        pltpu.VMEM((heads, dh), F32)],
    )
    return pl.pallas_call(
        functools.partial(_fox_sample_kernel, scale=dh ** -0.5, heads=heads, pages=pages),
        grid_spec=grid_spec,
        out_shape=jax.ShapeDtypeStruct((db, heads, dh), BF16),
        compiler_params=_params("parallel", "arbitrary"),
    )(page_table, q, kn, vn, fn_row, after_row, *([cache_k] * pages), *([cache_v] * pages))


def _s5_discretise(a_re, a_im, log_dt):
    dt = jnp.exp(log_dt)
    mag = jnp.exp(a_re * dt)
    ang = a_im * dt
    abar_re = mag * jnp.cos(ang)
    abar_im = mag * jnp.sin(ang)
    den = a_re * a_re + a_im * a_im
    num_re = abar_re - 1.0
    zoh_re = (num_re * a_re + abar_im * a_im) / den
    zoh_im = (abar_im * a_re - num_re * a_im) / den
    return abar_re, abar_im, zoh_re, zoh_im


def _cmul(ar, ai, br, bi):
    return ar * br - ai * bi, ar * bi + ai * br


def _s5_tables_kernel(ar_r, ai_r, ld_r, br_r, bi_r, ar_t, ai_t, ld_t, cr_t, ci_t, ar_f, ai_f, ld_f,
                      kt_ref, wt_ref, et_ref, w0_ref, e0_ref, ap_ref, *, chunk, gsz, psz):
    ngrp = LANES // gsz
    pw = ngrp * psz
    abr, abi, zr, zi = _s5_discretise(ar_r[...], ai_r[...], ld_r[...])
    bbr, bbi = _cmul(zr, zi, br_r[...], bi_r[...])
    atr, ati, _, _ = _s5_discretise(ar_t[...], ai_t[...], ld_t[...])
    ctr, cti = cr_t[...], ci_t[...]
    afr, afi, _, _ = _s5_discretise(ar_f[...], ai_f[...], ld_f[...])

    def group_of(shape, axis, size):
        return _idiv(lax.broadcasted_iota(jnp.int32, shape, axis), size)

    same_kk = group_of((LANES, LANES), 0, gsz) == group_of((LANES, LANES), 1, gsz)
    same_wp = group_of((LANES, pw), 0, gsz) == group_of((LANES, pw), 1, psz)
    same_pe = group_of((pw, LANES), 0, psz) == group_of((pw, LANES), 1, gsz)
    rep = jnp.where(_imod(lax.broadcasted_iota(jnp.int32, (psz, pw), 1), psz)
                    == lax.broadcasted_iota(jnp.int32, (psz, pw), 0), 1.0, 0.0).astype(F32)

    def w_table(wr, wi):
        return jnp.concatenate([jnp.where(same_wp, _dot_f32(wr, rep), 0.0),
                                jnp.where(same_wp, _dot_f32(wi, rep), 0.0)], axis=1)

    def e_table(qr, qi):
        return jnp.concatenate([jnp.where(same_pe, jnp.tile(qr, (ngrp, 1)), 0.0),
                                jnp.where(same_pe, jnp.tile(-qi, (ngrp, 1)), 0.0)], axis=0)

    w0_ref[...] = w_table(bbr, bbi)
    e0_ref[...] = e_table(ctr, cti)
    ap_ref[0:1, :] = jnp.concatenate([afr, afi], axis=1)

    pr_r, pi_r = jnp.ones_like(abr), jnp.zeros_like(abr)
    pr_t, pi_t = jnp.ones_like(atr), jnp.zeros_like(atr)
    pr_f, pi_f = jnp.ones_like(afr), jnp.zeros_like(afr)
    for tau in range(chunk):
        wr, wi = _cmul(pr_r, pi_r, bbr, bbi)
        kt_ref[tau] = jnp.where(same_kk, _dot_f32(wr, ctr) - _dot_f32(wi, cti), 0.0).astype(kt_ref.dtype)
        wt_ref[tau] = w_table(wr, wi).astype(wt_ref.dtype)
        pr_r, pi_r = _cmul(pr_r, pi_r, abr, abi)
        pr_t, pi_t = _cmul(pr_t, pi_t, atr, ati)
        pr_f, pi_f = _cmul(pr_f, pi_f, afr, afi)
        qr, qi = _cmul(ctr, cti, pr_t, pi_t)
        et_ref[tau] = e_table(qr, qi).astype(et_ref.dtype)
    ap_ref[1:2, :] = jnp.concatenate([pr_f, pi_f], axis=1)


def s5_tables(a_re, a_im, log_dt, b_re, b_im, c_re, c_im):
    g, p = a_re.shape
    c = b_re.shape[2]
    assert LANES % c == 0 and (g * c) % LANES == 0
    nl = g * c // LANES
    ngrp = LANES // c
    pw = ngrp * p
    chunk = S5_CHUNK
    ld = jnp.broadcast_to(log_dt[:, None], (g, p))
    rows = lambda a: jnp.repeat(a, c, axis=0)
    lanes = lambda a: jnp.repeat(a.T, c, axis=1)
    flat = lambda a: a.reshape(1, g * p)
    args = [rows(a_re), rows(a_im), rows(ld),
            b_re.transpose(0, 2, 1).reshape(g * c, p), b_im.transpose(0, 2, 1).reshape(g * c, p),
            lanes(a_re), lanes(a_im), lanes(ld),
            c_re.transpose(2, 0, 1).reshape(p, g * c), c_im.transpose(2, 0, 1).reshape(p, g * c),
            flat(a_re), flat(a_im), flat(ld)]
    spec_r = pl.BlockSpec((LANES, p), lambda i: (i, 0))
    spec_t = pl.BlockSpec((p, LANES), lambda i: (0, i))
    spec_f = pl.BlockSpec((1, pw), lambda i: (0, i))
    return pl.pallas_call(
        functools.partial(_s5_tables_kernel, chunk=chunk, gsz=c, psz=p),
        grid=(nl,),
        in_specs=[spec_r] * 5 + [spec_t] * 5 + [spec_f] * 3,
        out_specs=[pl.BlockSpec((None, chunk, LANES, LANES), lambda i: (i, 0, 0, 0)),
                   pl.BlockSpec((None, chunk, LANES, 2 * pw), lambda i: (i, 0, 0, 0)),
                   pl.BlockSpec((None, chunk, 2 * pw, LANES), lambda i: (i, 0, 0, 0)),
                   pl.BlockSpec((None, LANES, 2 * pw), lambda i: (i, 0, 0)),
                   pl.BlockSpec((None, 2 * pw, LANES), lambda i: (i, 0, 0)),
                   pl.BlockSpec((None, 2, 2 * pw), lambda i: (i, 0, 0))],
        out_shape=[jax.ShapeDtypeStruct((nl, chunk, LANES, LANES), BF16),
                   jax.ShapeDtypeStruct((nl, chunk, LANES, 2 * pw), BF16),
                   jax.ShapeDtypeStruct((nl, chunk, 2 * pw, LANES), BF16),
                   jax.ShapeDtypeStruct((nl, LANES, 2 * pw), F32),
                   jax.ShapeDtypeStruct((nl, 2 * pw, LANES), F32),
                   jax.ShapeDtypeStruct((nl, 2, 2 * pw), F32)],
        compiler_params=_params("parallel"),
    )(*args)


def _s5_prompt_kernel(u_ref, d_ref, kt_ref, wt_ref, et_ref, ap_ref, z_ref, hre_ref, him_ref, s_sc,
                      *, chunk, bsz):
    m = u_ref.shape[0]
    rows = m // chunk
    nchunk = rows // bsz
    pw = ap_ref.shape[1] // 2
    u = [u_ref[pl.ds(i, rows, stride=chunk), :] for i in range(chunk)]
    ub = [x.astype(BF16) for x in u]
    s = _dot(ub[0], wt_ref[chunk - 1])
    for i in range(1, chunk):
        s = s + _dot(ub[i], wt_ref[chunk - 1 - i])
    s_sc[...] = s
    a_r = ap_ref[1:2, 0:pw]
    a_i = ap_ref[1:2, pw:2 * pw]
    for b in range(bsz):
        def body(g8, carry):
            hr, hi = carry
            base = pl.multiple_of(b * nchunk + g8 * SUBLANES, SUBLANES)
            tile = s_sc[pl.ds(base, SUBLANES), :]
            before_r, before_i = [], []
            for r in range(SUBLANES):
                before_r.append(hr)
                before_i.append(hi)
                nr = a_r * hr - a_i * hi + tile[r:r + 1, 0:pw]
                ni = a_r * hi + a_i * hr + tile[r:r + 1, pw:2 * pw]
                hr, hi = nr, ni
            s_sc[pl.ds(base, SUBLANES), 0:pw] = jnp.concatenate(before_r, axis=0)
            s_sc[pl.ds(base, SUBLANES), pw:2 * pw] = jnp.concatenate(before_i, axis=0)
            return hr, hi

        zero = jnp.zeros((1, pw), F32)
        hr, hi = lax.fori_loop(0, nchunk // SUBLANES, body, (zero, zero))
        hre_ref[b:b + 1, :] = hr
        him_ref[b:b + 1, :] = hi
    h_in = s_sc[...].astype(BF16)
    d = d_ref[...]
    for i in range(chunk):
        y = _dot(h_in, et_ref[i])
        for j in range(i + 1):
            y = y + _dot(ub[j], kt_ref[i - j])
        y = y + d * u[i]
        z_ref[pl.ds(i, rows, stride=chunk), :] = _gelu_tanh(y)


def s5_prompt(u, d_skip, kt, wt, et, ap, bsz):
    m, width = u.shape
    nl = width // LANES
    chunk = kt.shape[1]
    pw2 = ap.shape[2]
    rows = m // chunk
    assert (m // bsz) % (chunk * SUBLANES) == 0
    col = pl.BlockSpec((m, LANES), lambda i: (0, i))
    tab = lambda a: pl.BlockSpec((None,) + a.shape[1:], lambda i: (i,) + (0,) * (a.ndim - 1))
    st = pl.BlockSpec((bsz, pw2 // 2), lambda i: (0, i))
    return pl.pallas_call(
        functools.partial(_s5_prompt_kernel, chunk=chunk, bsz=bsz),
        grid=(nl,),
        in_specs=[col, pl.BlockSpec((1, LANES), lambda i: (0, i)), tab(kt), tab(wt), tab(et), tab(ap)],
        out_specs=[col, st, st],
        out_shape=[jax.ShapeDtypeStruct((m, width), F32),
                   jax.ShapeDtypeStruct((bsz, nl * pw2 // 2), F32),
                   jax.ShapeDtypeStruct((bsz, nl * pw2 // 2), F32)],
        scratch_shapes=[pltpu.VMEM((rows, pw2), F32)],
        compiler_params=_params("parallel"),
    )(u, d_skip.reshape(1, width), kt, wt, et, ap)


def _s5_sample_kernel(u_ref, d_ref, hre_ref, him_ref, w0_ref, e0_ref, ap_ref, z_ref, ore_ref, oim_ref):
    pw = hre_ref.shape[1]
    u = u_ref[...]
    bu = _dot_f32(u, w0_ref[...])
    a_r = ap_ref[0:1, 0:pw]
    a_i = ap_ref[0:1, pw:2 * pw]
    hr, hi = hre_ref[...], him_ref[...]
    nr = a_r * hr - a_i * hi + bu[:, 0:pw]
    ni = a_r * hi + a_i * hr + bu[:, pw:2 * pw]
    y = _dot_f32(jnp.concatenate([nr, ni], axis=1), e0_ref[...]) + d_ref[...] * u
    z_ref[...] = _gelu_tanh(y)
    ore_ref[...] = nr
    oim_ref[...] = ni


def s5_sample(u, d_skip, h_re, h_im, w0, e0, ap):
    m, width = u.shape
    nl = width // LANES
    pw2 = ap.shape[2]
    col = pl.BlockSpec((m, LANES), lambda i: (0, i))
    tab = lambda a: pl.BlockSpec((None,) + a.shape[1:], lambda i: (i,) + (0,) * (a.ndim - 1))
    st = pl.BlockSpec((m, pw2 // 2), lambda i: (0, i))
    return pl.pallas_call(
        _s5_sample_kernel,
        grid=(nl,),
        in_specs=[col, pl.BlockSpec((1, LANES), lambda i: (0, i)), st, st, tab(w0), tab(e0), tab(ap)],
        out_specs=[col, st, st],
        out_shape=[jax.ShapeDtypeStruct((m, width), F32),
                   jax.ShapeDtypeStruct(h_re.shape, F32),
                   jax.ShapeDtypeStruct(h_im.shape, F32)],
        compiler_params=_params("parallel"),
    )(u, d_skip.reshape(1, width), h_re, h_im, w0, e0, ap)


def _layer(x, l, big_w, lw, kv, dims, prompt, sample_state):
    cw, fw, sw, heads, dh, d_ff = dims
    m, d = x.shape
    tm = min(1024, m)
    xn = rmsnorm(x, lw["norm_mix"], BF16)
    proj = lambda off, n, ep, dt, extras=(), stack=None, tn=min(512, cw): matmul_ws(
        xn, [(big_w["w_in_t"], off)], n, ep, list(extras), dt, tm, tn, l, transposed=True, stack=stack)
    kv_shape = (big_w["w_in_t"].shape[0], m, fw)
    f0 = 3 * cw + 3 * fw
    hbc = proj(0, 3 * cw, _ep_plain, F32)
    q = proj(3 * cw, fw, _ep_plain, BF16)
    k_all = proj(3 * cw + fw, fw, _ep_plain, F32, stack=(kv[0], l, kv_shape))
    v_all = proj(3 * cw + 2 * fw, fw, _ep_plain, F32, stack=(kv[1], l, kv_shape))
    logf = proj(f0, LANES, _ep_logf, F32, extras=[(lw["fox_bias"], "row", 0)], tn=LANES)
    u_c = proj(f0 + heads, sw, _ep_plain, F32)
    sg = proj(f0 + heads + sw, 3 * d, _ep_sigmoid, F32)

    kt, wt, et, w0, e0, ap = lw["s5_tables"]
    if prompt is not None:
        bsz, t = prompt
        y_a, conv_new = conv_prompt(hbc, lw["conv_wt"], bsz, t)
        f, ft = forget_cumsum(logf, bsz, t)
        y_b = fox_prompt(q, k_all, v_all, l, f, ft, bsz, t, heads, dh)
        z, h_re, h_im = s5_prompt(u_c, lw["ssm_d"], kt, wt, et, ap, bsz)
    else:
        st = sample_state
        buf = st["conv"]
        y_a, u_a = conv_sample(hbc, buf[:, 0], buf[:, 1], lw["conv_wt"])
        conv_new = jnp.stack([buf[:, 1], u_a], axis=1)
        lf_new = logf[:, :heads]
        ps = st["cache_k"].shape[2]
        after = forget_after(st["page_table"], st["cache_logf"], l)
        db, npg = st["page_table"].shape
        y_b = fox_sample(st["page_table"], q.reshape(m, heads, dh), k_all[l].reshape(m, heads, dh),
                         v_all[l].reshape(m, heads, dh), jnp.tile(lf_new, (1, ps)).reshape(db, 1, ps * heads),
                         after.reshape(db, npg, 1, ps * heads), st["cache_k"], st["cache_v"], l)
        y_b = y_b.reshape(m, fw)
        z, h_re, h_im = s5_sample(u_c, lw["ssm_d"], st["h_re"], st["h_im"], w0, e0, ap)
    y_c = matmul(z, [(big_w["w_glu_bf"], 0)], sw, _ep_glu, [(z, "tile", 0)], BF16, tm, 1024, layer=l)

    merged = gated_merge(y_a, y_b, y_c, big_w["w_branch"], l, sg, 512, 512)
    x = matmul_ws(merged, [(big_w["w_out"], 0)], d, _ep_residual, [(x, "tile", 0)], F32, tm, 512, l)
    xn2 = rmsnorm(x, lw["norm_ffn"], BF16)
    tf = 256
    hid = matmul_ws(xn2, [(big_w["w_up"], 0), (big_w["w_up"], d_ff // tf)], d_ff, _ep_swiglu, [], BF16,
                    tm, tf, l)
    x = matmul(hid, [(big_w["w_down_bf"], 0)], d, _ep_residual, [(x, "tile", 0)], F32, 512, 512, layer=l)
    return x, (k_all, v_all), logf, conv_new, h_re, h_im


def kernel(x_prompt, x_sample, cache_k, cache_v, cache_logf, state_conv, state_ssm_re, state_ssm_im,
           page_table, norm_mix, w_in, conv_w, fox_bias, ssm_a_re, ssm_a_im, ssm_log_dt, ssm_b_re,
           ssm_b_im, ssm_c_re, ssm_c_im, ssm_d, w_glu, w_branch, w_out, norm_ffn, w_up, w_down,
           norm_final):
    bp, t, d = x_prompt.shape
    bd, td, _ = x_sample.shape
    depth = w_in.shape[0]
    cw = conv_w.shape[1]
    heads = fox_bias.shape[1]
    dh = cache_k.shape[4]
    fw = heads * dh
    g, p = ssm_a_re.shape[1:]
    sw = ssm_d.shape[1]
    d_ff = w_down.shape[1]
    assert td == 1
    dims = (cw, fw, sw, heads, dh, d_ff)

    big_w = {"w_in_t": jnp.swapaxes(w_in, 1, 2), "w_branch": w_branch, "w_out": w_out, "w_up": w_up,
             "w_glu_bf": w_glu.astype(BF16), "w_down_bf": w_down.astype(BF16)}
    xp = x_prompt.reshape(bp * t, d)
    xs = x_sample.reshape(bd * td, d)
    kv_p = (None, None)
    kv_s = (None, None)
    outs_p, outs_s = [], []
    for l in range(depth):
        lw = {
            "norm_mix": norm_mix[l], "norm_ffn": norm_ffn[l],
            "fox_bias": jnp.pad(fox_bias[l], (0, LANES - heads)).reshape(1, LANES),
            "conv_wt": conv_w[l].T,
            "ssm_d": ssm_d[l],
            "s5_tables": s5_tables(ssm_a_re[l], ssm_a_im[l], ssm_log_dt[l], ssm_b_re[l], ssm_b_im[l],
                                   ssm_c_re[l], ssm_c_im[l]),
        }
        xp, kv_p, lfp, cvp, hrp, hip = _layer(xp, l, big_w, lw, kv_p, dims, (bp, t), None)
        outs_p.append((lfp[:, :heads].reshape(bp, t, heads), cvp, hrp.reshape(bp, g, p), hip.reshape(bp, g, p)))
        st = {"conv": state_conv[l], "h_re": state_ssm_re[l].reshape(bd, g * p),
              "h_im": state_ssm_im[l].reshape(bd, g * p), "page_table": page_table,
              "cache_k": cache_k, "cache_v": cache_v, "cache_logf": cache_logf}
        xs, kv_s, lfs, cvs, hrs, his = _layer(xs, l, big_w, lw, kv_s, dims, None, st)
        outs_s.append((lfs[:, :heads].reshape(bd, td, heads), cvs, hrs.reshape(bd, g, p), his.reshape(bd, g, p)))
    stack = lambda outs: [jnp.stack(a, axis=0) for a in zip(*outs)]
    y_prompt = rmsnorm(xp, norm_final, F32).reshape(bp, t, d)
    y_sample = rmsnorm(xs, norm_final, F32).reshape(bd, td, d)
    kv5 = lambda a, b, n: a.reshape(depth, b, n, heads, dh)
    return (y_prompt, y_sample, kv5(kv_p[0], bp, t), kv5(kv_p[1], bp, t), *stack(outs_p),
            kv5(kv_s[0], bd, td), kv5(kv_s[1], bd, td), *stack(outs_s))
```

```python
import functools
import math

import jax
import jax.numpy as jnp
from jax import lax
from jax.experimental import pallas as pl
from jax.experimental.pallas import tpu as pltpu

F32 = jnp.float32
BF16 = jnp.bfloat16
HIGHEST = lax.Precision.HIGHEST

RMS_EPS = 1e-6
LOG2E = math.log2(math.e)
LANES = 128
SUBLANES = 8
VMEM_LIMIT = 60 * 1024 * 1024
S5_CHUNK = 8
NEG_BIG = -1e30
FOX_ROW_CHUNK = 32
ATTN_PAGES_PER_STEP = 4
AFTER_PAGES_PER_STEP = 16


def _params(*sem):
    return pltpu.CompilerParams(dimension_semantics=sem, vmem_limit_bytes=VMEM_LIMIT)


def _sigmoid(x):
    return 1.0 / (1.0 + jnp.exp(-x))


def _log_sigmoid(x):
    t = -x
    return -(jnp.maximum(t, 0.0) + jnp.log1p(jnp.exp(-jnp.abs(t))))


def _gelu_tanh(x):
    c = math.sqrt(2.0 / math.pi)
    return x * (0.5 * (1.0 + jnp.tanh(c * (x + 0.044715 * (x * x * x)))))


def _idiv(x, n):
    assert n & (n - 1) == 0
    return x >> (n.bit_length() - 1)


def _imod(x, n):
    assert n & (n - 1) == 0
    return x & (n - 1)


def _dot(a, b):
    return jnp.dot(a, b, preferred_element_type=F32)


def _dot_f32(a, b):
    return jnp.dot(a, b, preferred_element_type=F32, precision=HIGHEST)


def _rmsnorm_kernel(x_ref, g_ref, o_ref):
    x = x_ref[...]
    ms = jnp.mean(x * x, axis=-1, keepdims=True)
    o_ref[...] = (x * lax.rsqrt(ms + RMS_EPS) * g_ref[...]).astype(o_ref.dtype)


def rmsnorm(x, g, out_dtype):
    m, d = x.shape
    tm = min(256, m)
    return pl.pallas_call(
        _rmsnorm_kernel,
        grid=(m // tm,),
        in_specs=[pl.BlockSpec((tm, d), lambda i: (i, 0)),
                  pl.BlockSpec((1, d), lambda i: (0, 0))],
        out_specs=pl.BlockSpec((tm, d), lambda i: (i, 0)),
        out_shape=jax.ShapeDtypeStruct((m, d), out_dtype),
        compiler_params=_params("parallel"),
    )(x, g.reshape(1, d))


def _mm_kernel(*refs, n_w, epilogue):
    x_ref = refs[0]
    w_refs = refs[1:1 + n_w]
    e_refs = refs[1 + n_w:-1]
    o_ref = refs[-1]
    x = x_ref[...]
    if x.dtype != BF16:
        x = x.astype(BF16)
    accs = [_dot(x, w[...]) for w in w_refs]
    o_ref[...] = epilogue(accs, [e[...] for e in e_refs]).astype(o_ref.dtype)


def matmul(x, ws, n, epilogue, extras, out_dtype, tm, tn, layer=None):
    m, k = x.shape
    tm = min(tm, m)
    tn = min(tn, n)
    assert m % tm == 0 and n % tn == 0
    in_specs = [pl.BlockSpec((tm, k), lambda i, j: (i, 0))]
    args = [x]
    for w, c0 in ws:
        if w.ndim == 3:
            in_specs.append(pl.BlockSpec((None, k, tn), lambda i, j, c0=c0: (layer, 0, c0 + j)))
        else:
            in_specs.append(pl.BlockSpec((k, tn), lambda i, j, c0=c0: (0, c0 + j)))
        args.append(w)
    for a, kind, c0 in extras:
        if kind == "tile":
            in_specs.append(pl.BlockSpec((tm, tn), lambda i, j, c0=c0: (i, c0 + j)))
        else:
            in_specs.append(pl.BlockSpec((1, tn), lambda i, j, c0=c0: (0, c0 + j)))
        args.append(a)
    return pl.pallas_call(
        functools.partial(_mm_kernel, n_w=len(ws), epilogue=epilogue),
        grid=(m // tm, n // tn),
        in_specs=in_specs,
        out_specs=pl.BlockSpec((tm, tn), lambda i, j: (i, j)),
        out_shape=jax.ShapeDtypeStruct((m, n), out_dtype),
        compiler_params=_params("parallel", "parallel"),
    )(*args)


def _mmw_kernel(*refs, n_w, kinds, n_in, transposed, epilogue):
    x_ref = refs[0]
    w_refs = refs[1:1 + n_w]
    e_refs = refs[1 + n_w:1 + n_w + len(kinds)]
    n_side = sum(kind == "tile" for kind in kinds)
    xs_ref = refs[1 + n_w + len(kinds)]
    es_refs = list(refs[2 + n_w + len(kinds):2 + n_w + len(kinds) + n_side])
    o_ref, os_ref = refs[n_in], refs[n_in + 1]
    wb_refs = refs[-n_w:]

    def dots(x):
        if transposed:
            return [lax.dot_general(x, wb[...], (((1,), (1,)), ((), ())), preferred_element_type=F32)
                    for wb in wb_refs]
        return [_dot(x, wb[...]) for wb in wb_refs]

    @pl.when(pl.program_id(1) == 0)
    def _():
        for w, wb in zip(w_refs, wb_refs):
            wb[...] = (w[0] if transposed else w[...]).astype(BF16)
        extras_s = [es_refs.pop(0)[...] if kind == "tile" else e[...] for kind, e in zip(kinds, e_refs)]
        os_ref[...] = epilogue(dots(xs_ref[...]), extras_s).astype(os_ref.dtype)

    o_ref[...] = epilogue(dots(x_ref[...]), [e[...] for e in e_refs]).astype(o_ref.dtype)


def matmul_ws(x, xs, ws, n, epilogue, extras, out_dtype, tm, tn, layer, transposed=False, stack=None):
    m, k = x.shape
    ms = xs.shape[0]
    tm = min(tm, m)
    tn = min(tn, n)
    assert m % tm == 0 and n % tn == 0
    in_specs = [pl.BlockSpec((tm, k), lambda j, i: (i, 0))]
    args = [x]
    for w, off in ws:
        if transposed:
            in_specs.append(pl.BlockSpec((pl.Element(1), pl.Element(tn), pl.Element(k)),
                                         lambda j, i, off=off: (layer, pl.multiple_of(off + j * tn, SUBLANES), 0)))
        else:
            in_specs.append(pl.BlockSpec((None, k, tn), lambda j, i, off=off: (layer, 0, off + j)))
        args.append(w)
    for a, _, kind, c0 in extras:
        if kind == "tile":
            in_specs.append(pl.BlockSpec((tm, tn), lambda j, i, c0=c0: (i, c0 + j)))
        else:
            in_specs.append(pl.BlockSpec((1, tn), lambda j, i, c0=c0: (0, c0 + j)))
        args.append(a)
    in_specs.append(pl.BlockSpec((ms, k), lambda j, i: (0, 0)))
    args.append(xs)
    for _, a_s, kind, c0 in extras:
        if kind == "tile":
            in_specs.append(pl.BlockSpec((ms, tn), lambda j, i, c0=c0: (0, c0 + j)))
            args.append(a_s)
    n_in = len(args)
    aliases = {}
    if stack is None:
        out_specs = [pl.BlockSpec((tm, tn), lambda j, i: (i, j)), pl.BlockSpec((ms, tn), lambda j, i: (0, j))]
        out_shape = [jax.ShapeDtypeStruct((m, n), out_dtype), jax.ShapeDtypeStruct((ms, n), out_dtype)]
    else:
        bufs, slab, depth = stack
        out_specs = [pl.BlockSpec((None, tm, tn), lambda j, i: (slab, i, j)),
                     pl.BlockSpec((None, ms, tn), lambda j, i: (slab, 0, j))]
        out_shape = [jax.ShapeDtypeStruct((depth, m, n), out_dtype),
                     jax.ShapeDtypeStruct((depth, ms, n), out_dtype)]
        if bufs is not None:
            in_specs += [pl.BlockSpec(memory_space=pl.ANY)] * 2
            args += list(bufs)
            aliases = {n_in: 0, n_in + 1: 1}
    return pl.pallas_call(
        functools.partial(_mmw_kernel, n_w=len(ws), kinds=tuple(e[2] for e in extras), n_in=len(args),
                          transposed=transposed, epilogue=epilogue),
        grid=(n // tn, m // tm),
        in_specs=in_specs,
        out_specs=out_specs,
        out_shape=out_shape,
        scratch_shapes=[pltpu.VMEM((tn, k) if transposed else (k, tn), BF16) for _ in ws],
        input_output_aliases=aliases,
        compiler_params=_params("parallel", "arbitrary"),
    )(*args)


def _ep_plain(accs, extras):
    return accs[0]


def _ep_sigmoid(accs, extras):
    return _sigmoid(accs[0])


def _ep_logf(accs, extras):
    return _log_sigmoid(accs[0] + extras[0])


def _ep_residual(accs, extras):
    return extras[0] + accs[0]


def _ep_glu(accs, extras):
    return extras[0] * _sigmoid(accs[0])


def _ep_swiglu(accs, extras):
    g = accs[0]
    return (g * _sigmoid(g)) * accs[1]


def _merge_kernel(w_ref, *refs):
    main, side, (o_ref, os_ref, wb_ref) = refs[0:6], refs[6:12], refs[12:]

    def merge(ya_ref, yb_ref, yc_ref, ga_ref, gb_ref, gc_ref, out_ref):
        ka = ya_ref.shape[1]
        kb = ka + yb_ref.shape[1]
        a = _dot(ya_ref[...], wb_ref[0:ka, :])
        b = _dot(yb_ref[...], wb_ref[ka:kb, :])
        c = _dot(yc_ref[...], wb_ref[kb:, :])
        out_ref[...] = (ga_ref[...] * a + gb_ref[...] * b + gc_ref[...] * c).astype(out_ref.dtype)

    @pl.when(pl.program_id(1) == 0)
    def _():
        wb_ref[...] = w_ref[...].astype(BF16)
        merge(*side, os_ref)

    merge(*main, o_ref)


def gated_merge(ys, ys_side, w_branch, layer, sg, sg_side, tm, tn):
    m = ys[0].shape[0]
    ms = ys_side[0].shape[0]
    _, mix, d = w_branch.shape
    tm = min(tm, m)
    tn = min(tn, d)
    nj = d // tn
    row = lambda a, r, s: pl.BlockSpec((r, a.shape[1]), lambda j, i: (i * s, 0))
    gate = lambda c, r, s: pl.BlockSpec((r, tn), lambda j, i: (i * s, c * nj + j))
    group = lambda y, r, s: [row(a, r, s) for a in y] + [gate(c, r, s) for c in range(3)]
    return pl.pallas_call(
        _merge_kernel,
        grid=(nj, m // tm),
        in_specs=([pl.BlockSpec((None, mix, tn), lambda j, i: (layer, 0, j))]
                  + group(ys, tm, 1) + group(ys_side, ms, 0)),
        out_specs=[pl.BlockSpec((tm, tn), lambda j, i: (i, j)), pl.BlockSpec((ms, tn), lambda j, i: (0, j))],
        out_shape=[jax.ShapeDtypeStruct((m, d), BF16), jax.ShapeDtypeStruct((ms, d), BF16)],
        scratch_shapes=[pltpu.VMEM((mix, tn), BF16)],
        compiler_params=_params("parallel", "arbitrary"),
    )(w_branch, *ys, sg, sg, sg, *ys_side, sg_side, sg_side, sg_side)


def _conv_prompt_kernel(h_ref, b_ref, c_ref, w_ref, y_ref, st_ref):
    u = c_ref[...] * h_ref[...]
    t = u.shape[0]
    row = lax.broadcasted_iota(jnp.int32, u.shape, 0)
    u1 = jnp.where(row >= 1, pltpu.roll(u, 1, 0), 0.0)
    u2 = jnp.where(row >= 2, pltpu.roll(u, 2, 0), 0.0)
    w = w_ref[...]
    conv = u2 * w[0:1, :] + u1 * w[1:2, :] + u * w[2:3, :]
    y_ref[...] = (b_ref[...] * conv).astype(y_ref.dtype)
    st_ref[...] = u[t - 2:t, :]


def conv_prompt(hbc, wt, bsz, t):
    cw = wt.shape[1]
    tc = min(256, cw)
    nc = cw // tc
    hbc3 = hbc.reshape(bsz, t, 3 * cw)
    spec = lambda off: pl.BlockSpec((None, t, tc), lambda b, c, off=off: (b, 0, off * nc + c))
    y, st = pl.pallas_call(
        _conv_prompt_kernel,
        grid=(bsz, nc),
        in_specs=[spec(0), spec(1), spec(2), pl.BlockSpec((3, tc), lambda b, c: (0, c))],
        out_specs=[pl.BlockSpec((None, t, tc), lambda b, c: (b, 0, c)),
                   pl.BlockSpec((None, 2, tc), lambda b, c: (b, 0, c))],
        out_shape=[jax.ShapeDtypeStruct((bsz, t, cw), BF16),
                   jax.ShapeDtypeStruct((bsz, 2, cw), F32)],
        compiler_params=_params("parallel", "parallel"),
    )(hbc3, hbc3, hbc3, wt)
    return y.reshape(bsz * t, cw), st


def _conv_sample_kernel(h_ref, b_ref, c_ref, buf0_ref, buf1_ref, w_ref, y_ref, u_ref):
    u = c_ref[...] * h_ref[...]
    w = w_ref[...]
    conv = buf0_ref[...] * w[0:1, :] + buf1_ref[...] * w[1:2, :] + u * w[2:3, :]
    y_ref[...] = (b_ref[...] * conv).astype(y_ref.dtype)
    u_ref[...] = u


def conv_sample(hbc, buf0, buf1, wt):
    m, cw = buf0.shape
    spec = lambda off: pl.BlockSpec((m, cw), lambda i, off=off: (0, off))
    return pl.pallas_call(
        _conv_sample_kernel,
        grid=(1,),
        in_specs=[spec(0), spec(1), spec(2), spec(0), spec(0), pl.BlockSpec((3, cw), lambda i: (0, 0))],
        out_specs=[spec(0), spec(0)],
        out_shape=[jax.ShapeDtypeStruct((m, cw), BF16), jax.ShapeDtypeStruct((m, cw), F32)],
        compiler_params=_params("arbitrary"),
    )(hbc, hbc, hbc, buf0, buf1, wt)


def _fcum_kernel(lf_ref, f_ref, ft_ref):
    x = lf_ref[...]
    t = x.shape[0]
    row = lax.broadcasted_iota(jnp.int32, x.shape, 0)
    s = 1
    while s < t:
        x = x + jnp.where(row >= s, pltpu.roll(x, s, 0), 0.0)
        s *= 2
    f_ref[...] = x
    ft_ref[...] = x.T


def forget_cumsum(logf, bsz, t):
    lf3 = logf.reshape(bsz, t, LANES)
    return pl.pallas_call(
        _fcum_kernel,
        grid=(bsz,),
        in_specs=[pl.BlockSpec((None, t, LANES), lambda b: (b, 0, 0))],
        out_specs=[pl.BlockSpec((None, t, LANES), lambda b: (b, 0, 0)),
                   pl.BlockSpec((None, LANES, t), lambda b: (b, 0, 0))],
        out_shape=[jax.ShapeDtypeStruct((bsz, t, LANES), F32),
                   jax.ShapeDtypeStruct((bsz, LANES, t), F32)],
        compiler_params=_params("parallel"),
    )(lf3)


def _fox_prompt_kernel(q_ref, k_ref, v_ref, f_ref, ft_ref, o_ref, m_sc, acc_sc, fq_sc, s_sc, p_sc, pm_sc,
                       *, heads, dh, scale, tq, tk):
    qi = pl.program_id(1)
    ki = pl.program_id(2)
    rc = FOX_ROW_CHUNK
    nlb = tk // LANES

    @pl.when(ki == 0)
    def _():
        m_sc[...] = jnp.full(m_sc.shape, NEG_BIG, F32)
        acc_sc[...] = jnp.zeros(acc_sc.shape, F32)
        fq = f_ref[...] * LOG2E
        for h in range(heads):
            fq_sc[h] = jnp.broadcast_to(fq[:, h:h + 1], (tq, LANES))

    def block(diagonal):
        fk = ft_ref[...] * LOG2E
        ones = jnp.ones((tk, LANES), BF16)
        ahead = (lax.broadcasted_iota(jnp.int32, (rc, tk), 1) - lax.broadcasted_iota(jnp.int32, (rc, tk), 0))
        for h in range(heads):
            sl = slice(h * dh, (h + 1) * dh)
            buf = h % 2
            kh = k_ref[:, sl].astype(BF16)
            vh = jnp.concatenate([v_ref[:, sl].astype(BF16), ones], axis=1)
            s_sc[buf] = lax.dot_general(q_ref[:, sl], kh, (((1,), (1,)), ((), ())),
                                        preferred_element_type=F32)
            fk_h = fk[h:h + 1, :]
            for r0 in range(0, tq, rc):
                rows = slice(r0, r0 + rc)
                x = s_sc[buf, rows, :] * (scale * LOG2E) + jnp.tile(fq_sc[h, rows, :], (1, nlb)) - fk_h
                if diagonal:
                    x = jnp.where(ahead <= r0, x, NEG_BIG)
                s_sc[buf, rows, :] = x
                pm = x[:, 0:LANES]
                for j in range(1, nlb):
                    pm = jnp.maximum(pm, x[:, j * LANES:(j + 1) * LANES])
                pm_sc[buf, rows, :] = pm
            m_prev = m_sc[h]
            m_new = jnp.maximum(m_prev, jnp.max(pm_sc[buf], axis=1, keepdims=True))
            m_sc[h] = m_new
            pm_sc[buf] = m_new
            alpha = jnp.exp2(m_prev - m_new)
            for r0 in range(0, tq, rc):
                rows = slice(r0, r0 + rc)
                x = s_sc[buf, rows, :] - jnp.tile(pm_sc[buf, rows, :], (1, nlb))
                p_sc[buf, rows, :] = jnp.exp2(x).astype(BF16)
            acc_sc[h] = jnp.tile(alpha, (1, 2)) * acc_sc[h] + _dot(p_sc[buf], vh)

    @pl.when(ki < qi)
    def _():
        block(False)

    @pl.when(ki == qi)
    def _():
        block(True)
        for h in range(heads):
            acc = acc_sc[h]
            o_ref[:, h * dh:(h + 1) * dh] = (acc[:, 0:dh] / acc[:, dh:2 * dh]).astype(o_ref.dtype)


def fox_prompt(q, k, v, layer, f, ft, bsz, t, heads, dh):
    assert dh == LANES
    tq = tk = min(512, t)
    nq = t // tq
    width = heads * dh
    hp = min(LANES, -(-heads // SUBLANES) * SUBLANES)
    kernel = functools.partial(_fox_prompt_kernel, heads=heads, dh=dh, scale=dh ** -0.5, tq=tq, tk=tk)
    kv = pl.BlockSpec((None, tk, width), lambda b, i, j: (layer, b * nq + jnp.minimum(i, j), 0))
    return pl.pallas_call(
        kernel,
        grid=(bsz, nq, nq),
        in_specs=[
            pl.BlockSpec((tq, width), lambda b, i, j: (b * nq + i, 0)),
            kv, kv,
            pl.BlockSpec((None, tq, LANES), lambda b, i, j: (b, i, 0)),
            pl.BlockSpec((None, hp, tk), lambda b, i, j: (b, 0, jnp.minimum(i, j))),
        ],
        out_specs=pl.BlockSpec((tq, width), lambda b, i, j: (b * nq + i, 0)),
        out_shape=jax.ShapeDtypeStruct((bsz * t, width), BF16),
        scratch_shapes=[pltpu.VMEM((heads, tq, LANES), F32),
                        pltpu.VMEM((heads, tq, 2 * dh), F32),
                        pltpu.VMEM((heads, tq, LANES), F32),
                        pltpu.VMEM((2, tq, tk), F32),
                        pltpu.VMEM((2, tq, tk), BF16),
                        pltpu.VMEM((2, tq, LANES), F32)],
        compiler_params=_params("parallel", "parallel", "arbitrary"),
    )(q, k, v, f, ft)


def _after_kernel(pt_ref, *refs, pages):
    lf_refs = refs[:pages]
    o_ref, carry_sc = refs[pages], refs[pages + 1]

    @pl.when(pl.program_id(1) == 0)
    def _():
        carry_sc[...] = jnp.zeros(carry_sc.shape, F32)

    ps = lf_refs[0].shape[0]
    r = lax.broadcasted_iota(jnp.int32, (ps, ps), 0)
    c = lax.broadcasted_iota(jnp.int32, (ps, ps), 1)
    later = jnp.where(c > r, 1.0, 0.0).astype(F32)
    carry = carry_sc[...]
    for s in range(pages):
        lf = lf_refs[s][...]
        o_ref[pages - 1 - s] = _dot_f32(later, lf) + carry
        carry = carry + jnp.sum(lf, axis=0, keepdims=True)
    carry_sc[...] = carry


def forget_after(page_table, cache_logf, layer):
    db, npg = page_table.shape
    _, _, ps, heads = cache_logf.shape
    pages = math.gcd(npg, AFTER_PAGES_PER_STEP)
    nsteps = npg // pages
    lf_spec = lambda s: pl.BlockSpec((None, None, ps, heads),
                                     lambda b, p, pt: (layer, pt[b, npg - 1 - (p * pages + s)], 0, 0))
    grid_spec = pltpu.PrefetchScalarGridSpec(
        num_scalar_prefetch=1,
        grid=(db, nsteps),
        in_specs=[lf_spec(s) for s in range(pages)],
        out_specs=pl.BlockSpec((None, pages, ps, heads), lambda b, p, pt: (b, nsteps - 1 - p, 0, 0)),
        scratch_shapes=[pltpu.VMEM((1, heads), F32)],
    )
    return pl.pallas_call(
        functools.partial(_after_kernel, pages=pages),
        grid_spec=grid_spec,
        out_shape=jax.ShapeDtypeStruct((db, npg, ps, heads), F32),
        compiler_params=_params("parallel", "arbitrary"),
    )(page_table, *([cache_logf] * pages))


def _lane_class_reduce(x, op, period):
    s = period
    while s < LANES:
        x = op(x, pltpu.roll(x, s, 1))
        s *= 2
    return x


def _lane_to_col(x, heads):
    sub = lax.broadcasted_iota(jnp.int32, (heads, LANES), 0)
    lane = lax.broadcasted_iota(jnp.int32, (heads, LANES), 1)
    return jnp.sum(jnp.where(lane == sub, jnp.broadcast_to(x, (heads, LANES)), 0.0), axis=1, keepdims=True)


def _fox_sample_kernel(pt_ref, q_ref, kn_ref, vn_ref, fn_ref, aft_ref, *refs, scale, heads, pages):
    k_refs = refs[:pages]
    v_refs = refs[pages:2 * pages]
    o_ref, m_sc, l_sc, acc_sc = refs[2 * pages:]
    p = pl.program_id(1)
    ps, _, dh = k_refs[0].shape
    npg = ps * heads
    n = pages * npg
    nb = n // LANES

    @pl.when(p == 0)
    def _():
        m_sc[...] = jnp.full(m_sc.shape, NEG_BIG, F32)
        l_sc[...] = jnp.zeros(l_sc.shape, F32)
        acc_sc[...] = jnp.zeros(acc_sc.shape, F32)

    q = q_ref[...]
    k2 = jnp.concatenate([r[...].reshape(npg, dh).astype(BF16) for r in k_refs], axis=0)
    v2 = jnp.concatenate([r[...].reshape(npg, dh).astype(BF16) for r in v_refs], axis=0)
    st = lax.dot_general(q, k2, (((1,), (1,)), ((), ())), preferred_element_type=F32)
    sub = lax.broadcasted_iota(jnp.int32, (heads, n), 0)
    lane = lax.broadcasted_iota(jnp.int32, (heads, n), 1)
    own = _imod(lane, heads) == sub
    s = jnp.sum(jnp.where(own, st, 0.0), axis=0, keepdims=True) * scale
    bias = jnp.concatenate([aft_ref[j] for j in range(pages)], axis=1)
    s = s + jnp.tile(fn_ref[...], (1, pages)) + bias

    m_page = s[:, 0:LANES]
    for j in range(1, nb):
        m_page = jnp.maximum(m_page, s[:, j * LANES:(j + 1) * LANES])
    m_page = _lane_class_reduce(m_page, jnp.maximum, heads)
    m_prev = m_sc[...]
    m_new = jnp.maximum(m_prev, m_page)
    alpha = jnp.exp(m_prev - m_new)
    pr = jnp.exp(s - jnp.tile(m_new, (1, nb)))
    l_page = pr[:, 0:LANES]
    for j in range(1, nb):
        l_page = l_page + pr[:, j * LANES:(j + 1) * LANES]
    l_sc[...] = alpha * l_sc[...] + l_page
    m_sc[...] = m_new
    pm = jnp.where(own, jnp.broadcast_to(pr, (heads, n)), 0.0).astype(BF16)
    acc_sc[...] = acc_sc[...] * _lane_to_col(alpha, heads) + _dot(pm, v2)

    @pl.when(p == pl.num_programs(1) - 1)
    def _():
        s_new = jnp.sum(q.astype(F32) * kn_ref[...], axis=1, keepdims=True) * scale
        m_col = _lane_to_col(m_sc[...], heads)
        l_col = _lane_to_col(_lane_class_reduce(l_sc[...], jnp.add, heads), heads)
        m_fin = jnp.maximum(m_col, s_new)
        a_col = jnp.exp(m_col - m_fin)
        p_new = jnp.exp(s_new - m_fin)
        num = acc_sc[...] * a_col + p_new * vn_ref[...]
        o_ref[...] = (num / (l_col * a_col + p_new)).astype(o_ref.dtype)


def fox_sample(page_table, q, kn, vn, fn_row, after_row, cache_k, cache_v, layer):
    db, npg = page_table.shape
    _, _, ps, heads, dh = cache_k.shape
    n = ps * heads
    assert LANES % heads == 0 and n % LANES == 0
    pages = math.gcd(npg, ATTN_PAGES_PER_STEP)
    tok = pl.BlockSpec((None, heads, dh), lambda b, p, pt: (b, 0, 0))
    page = lambda s: pl.BlockSpec((None, None, ps, heads, dh),
                                  lambda b, p, pt: (layer, pt[b, p * pages + s], 0, 0, 0))
    grid_spec = pltpu.PrefetchScalarGridSpec(
        num_scalar_prefetch=1,
        grid=(db, npg // pages),
        in_specs=[tok, tok, tok,
                  pl.BlockSpec((None, 1, n), lambda b, p, pt: (b, 0, 0)),
                  pl.BlockSpec((None, pages, 1, n), lambda b, p, pt: (b, p, 0, 0))]
                 + [page(s) for s in range(pages)] * 2,
        out_specs=tok,
        scratch_shapes=[pltpu.VMEM((1, LANES), F32), pltpu.VMEM((1, LANES), F32),
                        pltpu.VMEM((heads, dh), F32)],
    )
    return pl.pallas_call(
        functools.partial(_fox_sample_kernel, scale=dh ** -0.5, heads=heads, pages=pages),
        grid_spec=grid_spec,
        out_shape=jax.ShapeDtypeStruct((db, heads, dh), BF16),
        compiler_params=_params("parallel", "arbitrary"),
    )(page_table, q, kn, vn, fn_row, after_row, *([cache_k] * pages), *([cache_v] * pages))


def _s5_discretise(a_re, a_im, log_dt):
    dt = jnp.exp(log_dt)
    mag = jnp.exp(a_re * dt)
    ang = a_im * dt
    abar_re = mag * jnp.cos(ang)
    abar_im = mag * jnp.sin(ang)
    den = a_re * a_re + a_im * a_im
    num_re = abar_re - 1.0
    zoh_re = (num_re * a_re + abar_im * a_im) / den
    zoh_im = (abar_im * a_re - num_re * a_im) / den
    return abar_re, abar_im, zoh_re, zoh_im


def _cmul(ar, ai, br, bi):
    return ar * br - ai * bi, ar * bi + ai * br


def _s5_tables_kernel(ar_r, ai_r, ld_r, br_r, bi_r, ar_t, ai_t, ld_t, cr_t, ci_t, ar_f, ai_f, ld_f,
                      kt_ref, wt_ref, et_ref, w0_ref, e0_ref, ap_ref, *, chunk, gsz, psz):
    ngrp = LANES // gsz
    pw = ngrp * psz
    abr, abi, zr, zi = _s5_discretise(ar_r[...], ai_r[...], ld_r[...])
    bbr, bbi = _cmul(zr, zi, br_r[...], bi_r[...])
    atr, ati, _, _ = _s5_discretise(ar_t[...], ai_t[...], ld_t[...])
    ctr, cti = cr_t[...], ci_t[...]
    afr, afi, _, _ = _s5_discretise(ar_f[...], ai_f[...], ld_f[...])

    def group_of(shape, axis, size):
        return _idiv(lax.broadcasted_iota(jnp.int32, shape, axis), size)

    same_kk = group_of((LANES, LANES), 0, gsz) == group_of((LANES, LANES), 1, gsz)
    same_wp = group_of((LANES, pw), 0, gsz) == group_of((LANES, pw), 1, psz)
    same_pe = group_of((pw, LANES), 0, psz) == group_of((pw, LANES), 1, gsz)
    rep = jnp.where(_imod(lax.broadcasted_iota(jnp.int32, (psz, pw), 1), psz)
                    == lax.broadcasted_iota(jnp.int32, (psz, pw), 0), 1.0, 0.0).astype(F32)

    def w_table(wr, wi):
        return jnp.concatenate([jnp.where(same_wp, _dot_f32(wr, rep), 0.0),
                                jnp.where(same_wp, _dot_f32(wi, rep), 0.0)], axis=1)

    def e_table(qr, qi):
        return jnp.concatenate([jnp.where(same_pe, jnp.tile(qr, (ngrp, 1)), 0.0),
                                jnp.where(same_pe, jnp.tile(-qi, (ngrp, 1)), 0.0)], axis=0)

    w0_ref[...] = w_table(bbr, bbi)
    e0_ref[...] = e_table(ctr, cti)
    ap_ref[0:1, :] = jnp.concatenate([afr, afi], axis=1)

    pr_r, pi_r = jnp.ones_like(abr), jnp.zeros_like(abr)
    pr_t, pi_t = jnp.ones_like(atr), jnp.zeros_like(atr)
    pr_f, pi_f = jnp.ones_like(afr), jnp.zeros_like(afr)
    for tau in range(chunk):
        wr, wi = _cmul(pr_r, pi_r, bbr, bbi)
        kt_ref[tau] = jnp.where(same_kk, _dot_f32(wr, ctr) - _dot_f32(wi, cti), 0.0).astype(kt_ref.dtype)
        wt_ref[tau] = w_table(wr, wi).astype(wt_ref.dtype)
        pr_r, pi_r = _cmul(pr_r, pi_r, abr, abi)
        pr_t, pi_t = _cmul(pr_t, pi_t, atr, ati)
        pr_f, pi_f = _cmul(pr_f, pi_f, afr, afi)
        qr, qi = _cmul(ctr, cti, pr_t, pi_t)
        et_ref[tau] = e_table(qr, qi).astype(et_ref.dtype)
    ap_ref[1:2, :] = jnp.concatenate([pr_f, pi_f], axis=1)


def s5_tables(a_re, a_im, log_dt, b_re, b_im, c_re, c_im):
    g, p = a_re.shape
    c = b_re.shape[2]
    assert LANES % c == 0 and (g * c) % LANES == 0
    nl = g * c // LANES
    ngrp = LANES // c
    pw = ngrp * p
    chunk = S5_CHUNK
    ld = jnp.broadcast_to(log_dt[:, None], (g, p))
    rows = lambda a: jnp.repeat(a, c, axis=0)
    lanes = lambda a: jnp.repeat(a.T, c, axis=1)
    flat = lambda a: a.reshape(1, g * p)
    args = [rows(a_re), rows(a_im), rows(ld),
            b_re.transpose(0, 2, 1).reshape(g * c, p), b_im.transpose(0, 2, 1).reshape(g * c, p),
            lanes(a_re), lanes(a_im), lanes(ld),
            c_re.transpose(2, 0, 1).reshape(p, g * c), c_im.transpose(2, 0, 1).reshape(p, g * c),
            flat(a_re), flat(a_im), flat(ld)]
    spec_r = pl.BlockSpec((LANES, p), lambda i: (i, 0))
    spec_t = pl.BlockSpec((p, LANES), lambda i: (0, i))
    spec_f = pl.BlockSpec((1, pw), lambda i: (0, i))
    return pl.pallas_call(
        functools.partial(_s5_tables_kernel, chunk=chunk, gsz=c, psz=p),
        grid=(nl,),
        in_specs=[spec_r] * 5 + [spec_t] * 5 + [spec_f] * 3,
        out_specs=[pl.BlockSpec((None, chunk, LANES, LANES), lambda i: (i, 0, 0, 0)),
                   pl.BlockSpec((None, chunk, LANES, 2 * pw), lambda i: (i, 0, 0, 0)),
                   pl.BlockSpec((None, chunk, 2 * pw, LANES), lambda i: (i, 0, 0, 0)),
                   pl.BlockSpec((None, LANES, 2 * pw), lambda i: (i, 0, 0)),
                   pl.BlockSpec((None, 2 * pw, LANES), lambda i: (i, 0, 0)),
                   pl.BlockSpec((None, 2, 2 * pw), lambda i: (i, 0, 0))],
        out_shape=[jax.ShapeDtypeStruct((nl, chunk, LANES, LANES), BF16),
                   jax.ShapeDtypeStruct((nl, chunk, LANES, 2 * pw), BF16),
                   jax.ShapeDtypeStruct((nl, chunk, 2 * pw, LANES), BF16),
                   jax.ShapeDtypeStruct((nl, LANES, 2 * pw), F32),
                   jax.ShapeDtypeStruct((nl, 2 * pw, LANES), F32),
                   jax.ShapeDtypeStruct((nl, 2, 2 * pw), F32)],
        compiler_params=_params("parallel"),
    )(*args)


def _s5_prompt_kernel(u_ref, d_ref, kt_ref, wt_ref, et_ref, ap_ref, z_ref, hre_ref, him_ref, s_sc,
                      *, chunk, bsz):
    m = u_ref.shape[0]
    rows = m // chunk
    nchunk = rows // bsz
    pw = ap_ref.shape[1] // 2
    u = [u_ref[pl.ds(i, rows, stride=chunk), :] for i in range(chunk)]
    ub = [x.astype(BF16) for x in u]
    s = _dot(ub[0], wt_ref[chunk - 1])
    for i in range(1, chunk):
        s = s + _dot(ub[i], wt_ref[chunk - 1 - i])
    s_sc[...] = s
    a_r = ap_ref[1:2, 0:pw]
    a_i = ap_ref[1:2, pw:2 * pw]
    def body(g8, carry):
        out = []
        for b in range(bsz):
            hr, hi = carry[2 * b], carry[2 * b + 1]
            base = pl.multiple_of(b * nchunk + g8 * SUBLANES, SUBLANES)
            tile = s_sc[pl.ds(base, SUBLANES), :]
            before_r, before_i = [], []
            for r in range(SUBLANES):
                before_r.append(hr)
                before_i.append(hi)
                nr = a_r * hr - a_i * hi + tile[r:r + 1, 0:pw]
                ni = a_r * hi + a_i * hr + tile[r:r + 1, pw:2 * pw]
                hr, hi = nr, ni
            s_sc[pl.ds(base, SUBLANES), 0:pw] = jnp.concatenate(before_r, axis=0)
            s_sc[pl.ds(base, SUBLANES), pw:2 * pw] = jnp.concatenate(before_i, axis=0)
            out += [hr, hi]
        return tuple(out)

    zero = jnp.zeros((1, pw), F32)
    last = lax.fori_loop(0, nchunk // SUBLANES, body, (zero,) * (2 * bsz))
    for b in range(bsz):
        hre_ref[b:b + 1, :] = last[2 * b]
        him_ref[b:b + 1, :] = last[2 * b + 1]
    h_in = s_sc[...].astype(BF16)
    d = d_ref[...]
    for i in range(chunk):
        y = _dot(h_in, et_ref[i])
        for j in range(i + 1):
            y = y + _dot(ub[j], kt_ref[i - j])
        y = y + d * u[i]
        z_ref[pl.ds(i, rows, stride=chunk), :] = _gelu_tanh(y)


def s5_prompt(u, d_skip, kt, wt, et, ap, bsz):
    m, width = u.shape
    nl = width // LANES
    chunk = kt.shape[1]
    pw2 = ap.shape[2]
    rows = m // chunk
    assert (m // bsz) % (chunk * SUBLANES) == 0
    col = pl.BlockSpec((m, LANES), lambda i: (0, i))
    tab = lambda a: pl.BlockSpec((None,) + a.shape[1:], lambda i: (i,) + (0,) * (a.ndim - 1))
    st = pl.BlockSpec((bsz, pw2 // 2), lambda i: (0, i))
    return pl.pallas_call(
        functools.partial(_s5_prompt_kernel, chunk=chunk, bsz=bsz),
        grid=(nl,),
        in_specs=[col, pl.BlockSpec((1, LANES), lambda i: (0, i)), tab(kt), tab(wt), tab(et), tab(ap)],
        out_specs=[col, st, st],
        out_shape=[jax.ShapeDtypeStruct((m, width), F32),
                   jax.ShapeDtypeStruct((bsz, nl * pw2 // 2), F32),
                   jax.ShapeDtypeStruct((bsz, nl * pw2 // 2), F32)],
        scratch_shapes=[pltpu.VMEM((rows, pw2), F32)],
        compiler_params=_params("parallel"),
    )(u, d_skip.reshape(1, width), kt, wt, et, ap)


def _s5_sample_kernel(u_ref, d_ref, hre_ref, him_ref, w0_ref, e0_ref, ap_ref, z_ref, ore_ref, oim_ref):
    pw = hre_ref.shape[1]
    u = u_ref[...]
    bu = _dot_f32(u, w0_ref[...])
    a_r = ap_ref[0:1, 0:pw]
    a_i = ap_ref[0:1, pw:2 * pw]
    hr, hi = hre_ref[...], him_ref[...]
    nr = a_r * hr - a_i * hi + bu[:, 0:pw]
    ni = a_r * hi + a_i * hr + bu[:, pw:2 * pw]
    y = _dot_f32(jnp.concatenate([nr, ni], axis=1), e0_ref[...]) + d_ref[...] * u
    z_ref[...] = _gelu_tanh(y)
    ore_ref[...] = nr
    oim_ref[...] = ni


def s5_sample(u, d_skip, h_re, h_im, w0, e0, ap):
    m, width = u.shape
    nl = width // LANES
    pw2 = ap.shape[2]
    col = pl.BlockSpec((m, LANES), lambda i: (0, i))
    tab = lambda a: pl.BlockSpec((None,) + a.shape[1:], lambda i: (i,) + (0,) * (a.ndim - 1))
    st = pl.BlockSpec((m, pw2 // 2), lambda i: (0, i))
    return pl.pallas_call(
        _s5_sample_kernel,
        grid=(nl,),
        in_specs=[col, pl.BlockSpec((1, LANES), lambda i: (0, i)), st, st, tab(w0), tab(e0), tab(ap)],
        out_specs=[col, st, st],
        out_shape=[jax.ShapeDtypeStruct((m, width), F32),
                   jax.ShapeDtypeStruct(h_re.shape, F32),
                   jax.ShapeDtypeStruct(h_im.shape, F32)],
        compiler_params=_params("parallel"),
    )(u, d_skip.reshape(1, width), h_re, h_im, w0, e0, ap)


def _layer(xp, xs, l, big_w, lw, kv, dims, prompt, st):
    cw, fw, sw, heads, dh, d_ff = dims
    m, d = xp.shape
    ms = xs.shape[0]
    depth = big_w["w_in_t"].shape[0]
    tm = min(1024, m)
    xn, xn_s = rmsnorm(xp, lw["norm_mix"], BF16), rmsnorm(xs, lw["norm_mix"], BF16)
    proj = lambda off, n, ep, dt, extras=(), stack=None, tn=min(512, cw): matmul_ws(
        xn, xn_s, [(big_w["w_in_t"], off)], n, ep, list(extras), dt, tm, tn, l, transposed=True, stack=stack)
    f0 = 3 * cw + 3 * fw
    hbc, hbc_s = proj(0, 3 * cw, _ep_plain, F32)
    q, q_s = proj(3 * cw, fw, _ep_plain, BF16)
    k_all = proj(3 * cw + fw, fw, _ep_plain, F32, stack=(kv and kv[0], l, depth))
    v_all = proj(3 * cw + 2 * fw, fw, _ep_plain, F32, stack=(kv and kv[1], l, depth))
    logf, logf_s = proj(f0, LANES, _ep_logf, F32, extras=[(lw["fox_bias"], None, "row", 0)], tn=LANES)
    u_c, u_c_s = proj(f0 + heads, sw, _ep_plain, F32)
    sg, sg_s = proj(f0 + heads + sw, 3 * d, _ep_sigmoid, F32)

    kt, wt, et, w0, e0, ap = lw["s5_tables"]
    bsz, t = prompt
    y_a, conv_new = conv_prompt(hbc, lw["conv_wt"], bsz, t)
    f, ft = forget_cumsum(logf, bsz, t)
    y_b = fox_prompt(q, k_all[0], v_all[0], l, f, ft, bsz, t, heads, dh)
    z, h_re, h_im = s5_prompt(u_c, lw["ssm_d"], kt, wt, et, ap, bsz)

    buf = st["conv"]
    y_a_s, u_a = conv_sample(hbc_s, buf[:, 0], buf[:, 1], lw["conv_wt"])
    conv_new_s = jnp.stack([buf[:, 1], u_a], axis=1)
    ps = st["cache_k"].shape[2]
    after = forget_after(st["page_table"], st["cache_logf"], l)
    db, npg = st["page_table"].shape
    y_b_s = fox_sample(st["page_table"], q_s.reshape(ms, heads, dh), k_all[1][l].reshape(ms, heads, dh),
                       v_all[1][l].reshape(ms, heads, dh),
                       jnp.tile(logf_s[:, :heads], (1, ps)).reshape(db, 1, ps * heads),
                       after.reshape(db, npg, 1, ps * heads), st["cache_k"], st["cache_v"], l)
    y_b_s = y_b_s.reshape(ms, fw)
    z_s, h_re_s, h_im_s = s5_sample(u_c_s, lw["ssm_d"], st["h_re"], st["h_im"], w0, e0, ap)

    glu = lambda zz: matmul(zz, [(big_w["w_glu_bf"], 0)], sw, _ep_glu, [(zz, "tile", 0)], BF16, tm, 1024, layer=l)
    merged, merged_s = gated_merge((y_a, y_b, glu(z)), (y_a_s, y_b_s, glu(z_s)), big_w["w_branch"], l,
                                   sg, sg_s, tm, 512)
    xp, xs = matmul_ws(merged, merged_s, [(big_w["w_out"], 0)], d, _ep_residual, [(xp, xs, "tile", 0)],
                       F32, tm, 512, l)
    xn2, xn2_s = rmsnorm(xp, lw["norm_ffn"], BF16), rmsnorm(xs, lw["norm_ffn"], BF16)
    tf = 256
    hid, hid_s = matmul_ws(xn2, xn2_s, [(big_w["w_up"], 0), (big_w["w_up"], d_ff // tf)], d_ff, _ep_swiglu,
                           [], BF16, 2 * tm, tf, l)
    down = lambda hh, xx: matmul(hh, [(big_w["w_down_bf"], 0)], d, _ep_residual, [(xx, "tile", 0)], F32,
                                 512, 512, layer=l)
    xp, xs = down(hid, xp), down(hid_s, xs)
    return (xp, xs, (k_all, v_all), (logf, conv_new, h_re, h_im), (logf_s, conv_new_s, h_re_s, h_im_s))


def kernel(x_prompt, x_sample, cache_k, cache_v, cache_logf, state_conv, state_ssm_re, state_ssm_im,
           page_table, norm_mix, w_in, conv_w, fox_bias, ssm_a_re, ssm_a_im, ssm_log_dt, ssm_b_re,
           ssm_b_im, ssm_c_re, ssm_c_im, ssm_d, w_glu, w_branch, w_out, norm_ffn, w_up, w_down,
           norm_final):
    bp, t, d = x_prompt.shape
    bd, td, _ = x_sample.shape
    depth = w_in.shape[0]
    cw = conv_w.shape[1]
    heads = fox_bias.shape[1]
    dh = cache_k.shape[4]
    fw = heads * dh
    g, p = ssm_a_re.shape[1:]
    sw = ssm_d.shape[1]
    d_ff = w_down.shape[1]
    assert td == 1
    dims = (cw, fw, sw, heads, dh, d_ff)

    big_w = {"w_in_t": jnp.swapaxes(w_in, 1, 2), "w_branch": w_branch, "w_out": w_out, "w_up": w_up,
             "w_glu_bf": w_glu.astype(BF16), "w_down_bf": w_down.astype(BF16)}
    xp = x_prompt.reshape(bp * t, d)
    xs = x_sample.reshape(bd * td, d)
    kv = None
    outs_p, outs_s = [], []
    for l in range(depth):
        lw = {
            "norm_mix": norm_mix[l], "norm_ffn": norm_ffn[l],
            "fox_bias": jnp.pad(fox_bias[l], (0, LANES - heads)).reshape(1, LANES),
            "conv_wt": conv_w[l].T,
            "ssm_d": ssm_d[l],
            "s5_tables": s5_tables(ssm_a_re[l], ssm_a_im[l], ssm_log_dt[l], ssm_b_re[l], ssm_b_im[l],
                                   ssm_c_re[l], ssm_c_im[l]),
        }
        st = {"conv": state_conv[l], "h_re": state_ssm_re[l].reshape(bd, g * p),
              "h_im": state_ssm_im[l].reshape(bd, g * p), "page_table": page_table,
              "cache_k": cache_k, "cache_v": cache_v, "cache_logf": cache_logf}
        xp, xs, kv, (lfp, cvp, hrp, hip), (lfs, cvs, hrs, his) = _layer(xp, xs, l, big_w, lw, kv, dims, (bp, t), st)
        outs_p.append((lfp[:, :heads].reshape(bp, t, heads), cvp, hrp.reshape(bp, g, p), hip.reshape(bp, g, p)))
        outs_s.append((lfs[:, :heads].reshape(bd, td, heads), cvs, hrs.reshape(bd, g, p), his.reshape(bd, g, p)))
    stack = lambda outs: [jnp.stack(a, axis=0) for a in zip(*outs)]
    y_prompt = rmsnorm(xp, norm_final, F32).reshape(bp, t, d)
    y_sample = rmsnorm(xs, norm_final, F32).reshape(bd, td, d)
    kv5 = lambda a, b, n: a.reshape(depth, b, n, heads, dh)
    (k_p, k_s), (v_p, v_s) = kv
    return (y_prompt, y_sample, kv5(k_p, bp, t), kv5(v_p, bp, t), *stack(outs_p),
            kv5(k_s, bd, td), kv5(v_s, bd, td), *stack(outs_s))
```

```python
import functools
import math

import jax
import jax.numpy as jnp
from jax import lax
from jax.experimental import pallas as pl
from jax.experimental.pallas import tpu as pltpu

F32 = jnp.float32
BF16 = jnp.bfloat16
HIGHEST = lax.Precision.HIGHEST

RMS_EPS = 1e-6
LOG2E = math.log2(math.e)
LANES = 128
SUBLANES = 8
VMEM_LIMIT = 60 * 1024 * 1024
S5_CHUNK = 8
NEG_BIG = -1e30
FOX_ROW_CHUNK = 32
ATTN_PAGES_PER_STEP = 8
AFTER_PAGES_PER_STEP = 16


def _params(*sem):
    return pltpu.CompilerParams(dimension_semantics=sem, vmem_limit_bytes=VMEM_LIMIT)


def _sigmoid(x):
    return 1.0 / (1.0 + jnp.exp(-x))


def _log_sigmoid(x):
    t = -x
    return -(jnp.maximum(t, 0.0) + jnp.log1p(jnp.exp(-jnp.abs(t))))


def _gelu_tanh(x):
    c = math.sqrt(2.0 / math.pi)
    return x * (0.5 * (1.0 + jnp.tanh(c * (x + 0.044715 * (x * x * x)))))


def _idiv(x, n):
    assert n & (n - 1) == 0
    return x >> (n.bit_length() - 1)


def _imod(x, n):
    assert n & (n - 1) == 0
    return x & (n - 1)


def _dot(a, b):
    return jnp.dot(a, b, preferred_element_type=F32)


def _dot_f32(a, b):
    return jnp.dot(a, b, preferred_element_type=F32, precision=HIGHEST)


def _rmsnorm_kernel(x_ref, g_ref, o_ref):
    x = x_ref[...]
    ms = jnp.mean(x * x, axis=-1, keepdims=True)
    o_ref[...] = (x * lax.rsqrt(ms + RMS_EPS) * g_ref[...]).astype(o_ref.dtype)


def rmsnorm(x, g, out_dtype):
    m, d = x.shape
    tm = min(512, m)
    return pl.pallas_call(
        _rmsnorm_kernel,
        grid=(m // tm,),
        in_specs=[pl.BlockSpec((tm, d), lambda i: (i, 0)),
                  pl.BlockSpec((1, d), lambda i: (0, 0))],
        out_specs=pl.BlockSpec((tm, d), lambda i: (i, 0)),
        out_shape=jax.ShapeDtypeStruct((m, d), out_dtype),
        compiler_params=_params("parallel"),
    )(x, g.reshape(1, d))


def _mm_kernel(*refs, n_w, epilogue):
    x_ref = refs[0]
    w_refs = refs[1:1 + n_w]
    e_refs = refs[1 + n_w:-1]
    o_ref = refs[-1]
    x = x_ref[...]
    if x.dtype != BF16:
        x = x.astype(BF16)
    accs = [_dot(x, w[...]) for w in w_refs]
    o_ref[...] = epilogue(accs, [e[...] for e in e_refs]).astype(o_ref.dtype)


def matmul(x, ws, n, epilogue, extras, out_dtype, tm, tn, layer=None):
    m, k = x.shape
    tm = min(tm, m)
    tn = min(tn, n)
    assert m % tm == 0 and n % tn == 0
    in_specs = [pl.BlockSpec((tm, k), lambda i, j: (i, 0))]
    args = [x]
    for w, c0 in ws:
        if w.ndim == 3:
            in_specs.append(pl.BlockSpec((None, k, tn), lambda i, j, c0=c0: (layer, 0, c0 + j)))
        else:
            in_specs.append(pl.BlockSpec((k, tn), lambda i, j, c0=c0: (0, c0 + j)))
        args.append(w)
    for a, kind, c0 in extras:
        if kind == "tile":
            in_specs.append(pl.BlockSpec((tm, tn), lambda i, j, c0=c0: (i, c0 + j)))
        else:
            in_specs.append(pl.BlockSpec((1, tn), lambda i, j, c0=c0: (0, c0 + j)))
        args.append(a)
    return pl.pallas_call(
        functools.partial(_mm_kernel, n_w=len(ws), epilogue=epilogue),
        grid=(m // tm, n // tn),
        in_specs=in_specs,
        out_specs=pl.BlockSpec((tm, tn), lambda i, j: (i, j)),
        out_shape=jax.ShapeDtypeStruct((m, n), out_dtype),
        compiler_params=_params("parallel", "parallel"),
    )(*args)


def _mmw_kernel(*refs, n_w, kinds, n_in, transposed, epilogue):
    x_ref = refs[0]
    w_refs = refs[1:1 + n_w]
    e_refs = refs[1 + n_w:1 + n_w + len(kinds)]
    n_side = sum(kind == "tile" for kind in kinds)
    xs_ref = refs[1 + n_w + len(kinds)]
    es_refs = list(refs[2 + n_w + len(kinds):2 + n_w + len(kinds) + n_side])
    o_ref, os_ref = refs[n_in], refs[n_in + 1]
    wb_refs = refs[-n_w:]

    def dots(x):
        if transposed:
            return [lax.dot_general(x, wb[...], (((1,), (1,)), ((), ())), preferred_element_type=F32)
                    for wb in wb_refs]
        return [_dot(x, wb[...]) for wb in wb_refs]

    @pl.when(pl.program_id(1) == 0)
    def _():
        for w, wb in zip(w_refs, wb_refs):
            wb[...] = (w[0] if transposed else w[...]).astype(BF16)
        extras_s = [es_refs.pop(0)[...] if kind == "tile" else e[...] for kind, e in zip(kinds, e_refs)]
        os_ref[...] = epilogue(dots(xs_ref[...]), extras_s).astype(os_ref.dtype)

    o_ref[...] = epilogue(dots(x_ref[...]), [e[...] for e in e_refs]).astype(o_ref.dtype)


def matmul_ws(x, xs, ws, n, epilogue, extras, out_dtype, tm, tn, layer, transposed=False, stack=None):
    m, k = x.shape
    ms = xs.shape[0]
    tm = min(tm, m)
    tn = min(tn, n)
    assert m % tm == 0 and n % tn == 0
    in_specs = [pl.BlockSpec((tm, k), lambda j, i: (i, 0))]
    args = [x]
    for w, off in ws:
        if transposed:
            in_specs.append(pl.BlockSpec((pl.Element(1), pl.Element(tn), pl.Element(k)),
                                         lambda j, i, off=off: (layer, pl.multiple_of(off + j * tn, SUBLANES), 0)))
        else:
            in_specs.append(pl.BlockSpec((None, k, tn), lambda j, i, off=off: (layer, 0, off + j)))
        args.append(w)
    for a, _, kind, c0 in extras:
        if kind == "tile":
            in_specs.append(pl.BlockSpec((tm, tn), lambda j, i, c0=c0: (i, c0 + j)))
        else:
            in_specs.append(pl.BlockSpec((1, tn), lambda j, i, c0=c0: (0, c0 + j)))
        args.append(a)
    in_specs.append(pl.BlockSpec((ms, k), lambda j, i: (0, 0)))
    args.append(xs)
    for _, a_s, kind, c0 in extras:
        if kind == "tile":
            in_specs.append(pl.BlockSpec((ms, tn), lambda j, i, c0=c0: (0, c0 + j)))
            args.append(a_s)
    n_in = len(args)
    aliases = {}
    if stack is None:
        out_specs = [pl.BlockSpec((tm, tn), lambda j, i: (i, j)), pl.BlockSpec((ms, tn), lambda j, i: (0, j))]
        out_shape = [jax.ShapeDtypeStruct((m, n), out_dtype), jax.ShapeDtypeStruct((ms, n), out_dtype)]
    else:
        bufs, slab, depth = stack
        out_specs = [pl.BlockSpec((None, tm, tn), lambda j, i: (slab, i, j)),
                     pl.BlockSpec((None, ms, tn), lambda j, i: (slab, 0, j))]
        out_shape = [jax.ShapeDtypeStruct((depth, m, n), out_dtype),
                     jax.ShapeDtypeStruct((depth, ms, n), out_dtype)]
        if bufs is not None:
            in_specs += [pl.BlockSpec(memory_space=pl.ANY)] * 2
            args += list(bufs)
            aliases = {n_in: 0, n_in + 1: 1}
    return pl.pallas_call(
        functools.partial(_mmw_kernel, n_w=len(ws), kinds=tuple(e[2] for e in extras), n_in=len(args),
                          transposed=transposed, epilogue=epilogue),
        grid=(n // tn, m // tm),
        in_specs=in_specs,
        out_specs=out_specs,
        out_shape=out_shape,
        scratch_shapes=[pltpu.VMEM((tn, k) if transposed else (k, tn), BF16) for _ in ws],
        input_output_aliases=aliases,
        compiler_params=_params("parallel", "arbitrary"),
    )(*args)


def _ep_plain(accs, extras):
    return accs[0]


def _ep_sigmoid(accs, extras):
    return _sigmoid(accs[0])


def _ep_logf(accs, extras):
    return _log_sigmoid(accs[0] + extras[0])


def _ep_residual(accs, extras):
    return extras[0] + accs[0]


def _ep_glu(accs, extras):
    return extras[0] * _sigmoid(accs[0])


def _ep_swiglu(accs, extras):
    g = accs[0]
    return (g * _sigmoid(g)) * accs[1]


def _merge_kernel(w_ref, *refs):
    main, side, (o_ref, os_ref, wb_ref) = refs[0:6], refs[6:12], refs[12:]

    def merge(ya_ref, yb_ref, yc_ref, ga_ref, gb_ref, gc_ref, out_ref):
        ka = ya_ref.shape[1]
        kb = ka + yb_ref.shape[1]
        a = _dot(ya_ref[...], wb_ref[0:ka, :])
        b = _dot(yb_ref[...], wb_ref[ka:kb, :])
        c = _dot(yc_ref[...], wb_ref[kb:, :])
        out_ref[...] = (ga_ref[...] * a + gb_ref[...] * b + gc_ref[...] * c).astype(out_ref.dtype)

    @pl.when(pl.program_id(1) == 0)
    def _():
        wb_ref[...] = w_ref[...].astype(BF16)
        merge(*side, os_ref)

    merge(*main, o_ref)


def gated_merge(ys, ys_side, w_branch, layer, sg, sg_side, tm, tn):
    m = ys[0].shape[0]
    ms = ys_side[0].shape[0]
    _, mix, d = w_branch.shape
    tm = min(tm, m)
    tn = min(tn, d)
    nj = d // tn
    row = lambda a, r, s: pl.BlockSpec((r, a.shape[1]), lambda j, i: (i * s, 0))
    gate = lambda c, r, s: pl.BlockSpec((r, tn), lambda j, i: (i * s, c * nj + j))
    group = lambda y, r, s: [row(a, r, s) for a in y] + [gate(c, r, s) for c in range(3)]
    return pl.pallas_call(
        _merge_kernel,
        grid=(nj, m // tm),
        in_specs=([pl.BlockSpec((None, mix, tn), lambda j, i: (layer, 0, j))]
                  + group(ys, tm, 1) + group(ys_side, ms, 0)),
        out_specs=[pl.BlockSpec((tm, tn), lambda j, i: (i, j)), pl.BlockSpec((ms, tn), lambda j, i: (0, j))],
        out_shape=[jax.ShapeDtypeStruct((m, d), BF16), jax.ShapeDtypeStruct((ms, d), BF16)],
        scratch_shapes=[pltpu.VMEM((mix, tn), BF16)],
        compiler_params=_params("parallel", "arbitrary"),
    )(w_branch, *ys, sg, sg, sg, *ys_side, sg_side, sg_side, sg_side)


def _conv_prompt_kernel(h_ref, b_ref, c_ref, w_ref, y_ref, st_ref):
    u = c_ref[...] * h_ref[...]
    t = u.shape[0]
    row = lax.broadcasted_iota(jnp.int32, u.shape, 0)
    u1 = jnp.where(row >= 1, pltpu.roll(u, 1, 0), 0.0)
    u2 = jnp.where(row >= 2, pltpu.roll(u, 2, 0), 0.0)
    w = w_ref[...]
    conv = u2 * w[0:1, :] + u1 * w[1:2, :] + u * w[2:3, :]
    y_ref[...] = (b_ref[...] * conv).astype(y_ref.dtype)
    st_ref[...] = u[t - 2:t, :]


def conv_prompt(hbc, wt, bsz, t):
    cw = wt.shape[1]
    tc = min(256, cw)
    nc = cw // tc
    hbc3 = hbc.reshape(bsz, t, 3 * cw)
    spec = lambda off: pl.BlockSpec((None, t, tc), lambda b, c, off=off: (b, 0, off * nc + c))
    y, st = pl.pallas_call(
        _conv_prompt_kernel,
        grid=(bsz, nc),
        in_specs=[spec(0), spec(1), spec(2), pl.BlockSpec((3, tc), lambda b, c: (0, c))],
        out_specs=[pl.BlockSpec((None, t, tc), lambda b, c: (b, 0, c)),
                   pl.BlockSpec((None, 2, tc), lambda b, c: (b, 0, c))],
        out_shape=[jax.ShapeDtypeStruct((bsz, t, cw), BF16),
                   jax.ShapeDtypeStruct((bsz, 2, cw), F32)],
        compiler_params=_params("parallel", "parallel"),
    )(hbc3, hbc3, hbc3, wt)
    return y.reshape(bsz * t, cw), st


def _conv_sample_kernel(h_ref, b_ref, c_ref, buf0_ref, buf1_ref, w_ref, y_ref, u_ref):
    u = c_ref[...] * h_ref[...]
    w = w_ref[...]
    conv = buf0_ref[...] * w[0:1, :] + buf1_ref[...] * w[1:2, :] + u * w[2:3, :]
    y_ref[...] = (b_ref[...] * conv).astype(y_ref.dtype)
    u_ref[...] = u


def conv_sample(hbc, buf0, buf1, wt):
    m, cw = buf0.shape
    spec = lambda off: pl.BlockSpec((m, cw), lambda i, off=off: (0, off))
    return pl.pallas_call(
        _conv_sample_kernel,
        grid=(1,),
        in_specs=[spec(0), spec(1), spec(2), spec(0), spec(0), pl.BlockSpec((3, cw), lambda i: (0, 0))],
        out_specs=[spec(0), spec(0)],
        out_shape=[jax.ShapeDtypeStruct((m, cw), BF16), jax.ShapeDtypeStruct((m, cw), F32)],
        compiler_params=_params("arbitrary"),
    )(hbc, hbc, hbc, buf0, buf1, wt)


def _fcum_kernel(lf_ref, f_ref, ft_ref):
    x = lf_ref[...]
    t = x.shape[0]
    row = lax.broadcasted_iota(jnp.int32, x.shape, 0)
    s = 1
    while s < t:
        x = x + jnp.where(row >= s, pltpu.roll(x, s, 0), 0.0)
        s *= 2
    f_ref[...] = x
    ft_ref[...] = x.T


def forget_cumsum(logf, bsz, t):
    lf3 = logf.reshape(bsz, t, LANES)
    return pl.pallas_call(
        _fcum_kernel,
        grid=(bsz,),
        in_specs=[pl.BlockSpec((None, t, LANES), lambda b: (b, 0, 0))],
        out_specs=[pl.BlockSpec((None, t, LANES), lambda b: (b, 0, 0)),
                   pl.BlockSpec((None, LANES, t), lambda b: (b, 0, 0))],
        out_shape=[jax.ShapeDtypeStruct((bsz, t, LANES), F32),
                   jax.ShapeDtypeStruct((bsz, LANES, t), F32)],
        compiler_params=_params("parallel"),
    )(lf3)


def _fox_prompt_kernel(q_ref, k_ref, v_ref, f_ref, ft_ref, o_ref, m_sc, acc_sc, fq_sc, s_sc, p_sc, pm_sc,
                       *, heads, dh, scale, tq, tk):
    qi = pl.program_id(1)
    ki = pl.program_id(2)
    rc = FOX_ROW_CHUNK
    nlb = tk // LANES

    @pl.when(ki == 0)
    def _():
        m_sc[...] = jnp.full(m_sc.shape, NEG_BIG, F32)
        acc_sc[...] = jnp.zeros(acc_sc.shape, F32)
        fq = f_ref[...] * LOG2E
        for h in range(heads):
            fq_sc[h] = jnp.broadcast_to(fq[:, h:h + 1], (tq, LANES))

    def block(diagonal):
        fk = ft_ref[...] * LOG2E
        ones = jnp.ones((tk, LANES), BF16)
        ahead = (lax.broadcasted_iota(jnp.int32, (rc, tk), 1) - lax.broadcasted_iota(jnp.int32, (rc, tk), 0))
        for h in range(heads):
            sl = slice(h * dh, (h + 1) * dh)
            buf = h % 2
            kh = k_ref[:, sl].astype(BF16)
            vh = jnp.concatenate([v_ref[:, sl].astype(BF16), ones], axis=1)
            s_sc[buf] = lax.dot_general(q_ref[:, sl], kh, (((1,), (1,)), ((), ())),
                                        preferred_element_type=F32)
            fk_h = fk[h:h + 1, :]
            for r0 in range(0, tq, rc):
                rows = slice(r0, r0 + rc)
                x = s_sc[buf, rows, :] * (scale * LOG2E) + jnp.tile(fq_sc[h, rows, :], (1, nlb)) - fk_h
                if diagonal:
                    x = jnp.where(ahead <= r0, x, NEG_BIG)
                s_sc[buf, rows, :] = x
                pm = x[:, 0:LANES]
                for j in range(1, nlb):
                    pm = jnp.maximum(pm, x[:, j * LANES:(j + 1) * LANES])
                pm_sc[buf, rows, :] = pm
            m_prev = m_sc[h]
            m_new = jnp.maximum(m_prev, jnp.max(pm_sc[buf], axis=1, keepdims=True))
            m_sc[h] = m_new
            pm_sc[buf] = m_new
            alpha = jnp.exp2(m_prev - m_new)
            for r0 in range(0, tq, rc):
                rows = slice(r0, r0 + rc)
                x = s_sc[buf, rows, :] - jnp.tile(pm_sc[buf, rows, :], (1, nlb))
                p_sc[buf, rows, :] = jnp.exp2(x).astype(BF16)
            acc_sc[h] = jnp.tile(alpha, (1, 2)) * acc_sc[h] + _dot(p_sc[buf], vh)

    @pl.when(ki < qi)
    def _():
        block(False)

    @pl.when(ki == qi)
    def _():
        block(True)
        for h in range(heads):
            acc = acc_sc[h]
            o_ref[:, h * dh:(h + 1) * dh] = (acc[:, 0:dh] / acc[:, dh:2 * dh]).astype(o_ref.dtype)


def fox_prompt(q, k, v, layer, f, ft, bsz, t, heads, dh):
    assert dh == LANES
    tq = tk = min(512, t)
    nq = t // tq
    width = heads * dh
    hp = min(LANES, -(-heads // SUBLANES) * SUBLANES)
    kernel = functools.partial(_fox_prompt_kernel, heads=heads, dh=dh, scale=dh ** -0.5, tq=tq, tk=tk)
    kv = pl.BlockSpec((None, tk, width), lambda b, i, j: (layer, b * nq + jnp.minimum(i, j), 0))
    return pl.pallas_call(
        kernel,
        grid=(bsz, nq, nq),
        in_specs=[
            pl.BlockSpec((tq, width), lambda b, i, j: (b * nq + i, 0)),
            kv, kv,
            pl.BlockSpec((None, tq, LANES), lambda b, i, j: (b, i, 0)),
            pl.BlockSpec((None, hp, tk), lambda b, i, j: (b, 0, jnp.minimum(i, j))),
        ],
        out_specs=pl.BlockSpec((tq, width), lambda b, i, j: (b * nq + i, 0)),
        out_shape=jax.ShapeDtypeStruct((bsz * t, width), BF16),
        scratch_shapes=[pltpu.VMEM((heads, tq, LANES), F32),
                        pltpu.VMEM((heads, tq, 2 * dh), F32),
                        pltpu.VMEM((heads, tq, LANES), F32),
                        pltpu.VMEM((2, tq, tk), F32),
                        pltpu.VMEM((2, tq, tk), BF16),
                        pltpu.VMEM((2, tq, LANES), F32)],
        compiler_params=_params("parallel", "parallel", "arbitrary"),
    )(q, k, v, f, ft)


def _after_kernel(pt_ref, *refs, pages):
    lf_refs = refs[:pages]
    o_ref, carry_sc = refs[pages], refs[pages + 1]

    @pl.when(pl.program_id(1) == 0)
    def _():
        carry_sc[...] = jnp.zeros(carry_sc.shape, F32)

    ps = lf_refs[0].shape[0]
    r = lax.broadcasted_iota(jnp.int32, (ps, ps), 0)
    c = lax.broadcasted_iota(jnp.int32, (ps, ps), 1)
    later = jnp.where(c > r, 1.0, 0.0).astype(F32)
    carry = carry_sc[...]
    for s in range(pages):
        lf = lf_refs[s][...]
        o_ref[pages - 1 - s] = _dot_f32(later, lf) + carry
        carry = carry + jnp.sum(lf, axis=0, keepdims=True)
    carry_sc[...] = carry


def forget_after(page_table, cache_logf, layer):
    db, npg = page_table.shape
    _, _, ps, heads = cache_logf.shape
    pages = math.gcd(npg, AFTER_PAGES_PER_STEP)
    nsteps = npg // pages
    lf_spec = lambda s: pl.BlockSpec((None, None, ps, heads),
                                     lambda b, p, pt: (layer, pt[b, npg - 1 - (p * pages + s)], 0, 0))
    grid_spec = pltpu.PrefetchScalarGridSpec(
        num_scalar_prefetch=1,
        grid=(db, nsteps),
        in_specs=[lf_spec(s) for s in range(pages)],
        out_specs=pl.BlockSpec((None, pages, ps, heads), lambda b, p, pt: (b, nsteps - 1 - p, 0, 0)),
        scratch_shapes=[pltpu.VMEM((1, heads), F32)],
    )
    return pl.pallas_call(
        functools.partial(_after_kernel, pages=pages),
        grid_spec=grid_spec,
        out_shape=jax.ShapeDtypeStruct((db, npg, ps, heads), F32),
        compiler_params=_params("parallel", "arbitrary"),
    )(page_table, *([cache_logf] * pages))


def _lane_class_reduce(x, op, period):
    s = period
    while s < LANES:
        x = op(x, pltpu.roll(x, s, 1))
        s *= 2
    return x


def _lane_to_col(x, heads):
    sub = lax.broadcasted_iota(jnp.int32, (heads, LANES), 0)
    lane = lax.broadcasted_iota(jnp.int32, (heads, LANES), 1)
    return jnp.sum(jnp.where(lane == sub, jnp.broadcast_to(x, (heads, LANES)), 0.0), axis=1, keepdims=True)


def _fox_sample_kernel(pt_ref, q_ref, kn_ref, vn_ref, fn_ref, aft_ref, *refs, scale, heads, pages):
    k_refs = refs[:pages]
    v_refs = refs[pages:2 * pages]
    o_ref, m_sc, l_sc, acc_sc = refs[2 * pages:]
    p = pl.program_id(1)
    ps, _, dh = k_refs[0].shape
    npg = ps * heads
    n = pages * npg
    nb = n // LANES

    @pl.when(p == 0)
    def _():
        m_sc[...] = jnp.full(m_sc.shape, NEG_BIG, F32)
        l_sc[...] = jnp.zeros(l_sc.shape, F32)
        acc_sc[...] = jnp.zeros(acc_sc.shape, F32)

    q = q_ref[...]
    k2 = jnp.concatenate([r[...].reshape(npg, dh).astype(BF16) for r in k_refs], axis=0)
    v2 = jnp.concatenate([r[...].reshape(npg, dh).astype(BF16) for r in v_refs], axis=0)
    st = lax.dot_general(q, k2, (((1,), (1,)), ((), ())), preferred_element_type=F32)
    sub = lax.broadcasted_iota(jnp.int32, (heads, n), 0)
    lane = lax.broadcasted_iota(jnp.int32, (heads, n), 1)
    own = _imod(lane, heads) == sub
    s = jnp.sum(jnp.where(own, st, 0.0), axis=0, keepdims=True) * scale
    bias = jnp.concatenate([aft_ref[j] for j in range(pages)], axis=1)
    s = s + jnp.tile(fn_ref[...], (1, pages)) + bias

    m_page = s[:, 0:LANES]
    for j in range(1, nb):
        m_page = jnp.maximum(m_page, s[:, j * LANES:(j + 1) * LANES])
    m_page = _lane_class_reduce(m_page, jnp.maximum, heads)
    m_prev = m_sc[...]
    m_new = jnp.maximum(m_prev, m_page)
    alpha = jnp.exp(m_prev - m_new)
    pr = jnp.exp(s - jnp.tile(m_new, (1, nb)))
    l_page = pr[:, 0:LANES]
    for j in range(1, nb):
        l_page = l_page + pr[:, j * LANES:(j + 1) * LANES]
    l_sc[...] = alpha * l_sc[...] + l_page
    m_sc[...] = m_new
    pm = jnp.where(own, jnp.broadcast_to(pr, (heads, n)), 0.0).astype(BF16)
    acc_sc[...] = acc_sc[...] * _lane_to_col(alpha, heads) + _dot(pm, v2)

    @pl.when(p == pl.num_programs(1) - 1)
    def _():
        s_new = jnp.sum(q.astype(F32) * kn_ref[...], axis=1, keepdims=True) * scale
        m_col = _lane_to_col(m_sc[...], heads)
        l_col = _lane_to_col(_lane_class_reduce(l_sc[...], jnp.add, heads), heads)
        m_fin = jnp.maximum(m_col, s_new)
        a_col = jnp.exp(m_col - m_fin)
        p_new = jnp.exp(s_new - m_fin)
        num = acc_sc[...] * a_col + p_new * vn_ref[...]
        o_ref[...] = (num / (l_col * a_col + p_new)).astype(o_ref.dtype)


def fox_sample(page_table, q, kn, vn, fn_row, after_row, cache_k, cache_v, layer):
    db, npg = page_table.shape
    _, _, ps, heads, dh = cache_k.shape
    n = ps * heads
    assert LANES % heads == 0 and n % LANES == 0
    pages = math.gcd(npg, ATTN_PAGES_PER_STEP)
    tok = pl.BlockSpec((None, heads, dh), lambda b, p, pt: (b, 0, 0))
    page = lambda s: pl.BlockSpec((None, None, ps, heads, dh),
                                  lambda b, p, pt: (layer, pt[b, p * pages + s], 0, 0, 0))
    grid_spec = pltpu.PrefetchScalarGridSpec(
        num_scalar_prefetch=1,
        grid=(db, npg // pages),
        in_specs=[tok, tok, tok,
                  pl.BlockSpec((None, 1, n), lambda b, p, pt: (b, 0, 0)),
                  pl.BlockSpec((None, pages, 1, n), lambda b, p, pt: (b, p, 0, 0))]
                 + [page(s) for s in range(pages)] * 2,
        out_specs=tok,
        scratch_shapes=[pltpu.VMEM((1, LANES), F32), pltpu.VMEM((1, LANES), F32),
                        pltpu.VMEM((heads, dh), F32)],
    )
    return pl.pallas_call(
        functools.partial(_fox_sample_kernel, scale=dh ** -0.5, heads=heads, pages=pages),
        grid_spec=grid_spec,
        out_shape=jax.ShapeDtypeStruct((db, heads, dh), BF16),
        compiler_params=_params("parallel", "arbitrary"),
    )(page_table, q, kn, vn, fn_row, after_row, *([cache_k] * pages), *([cache_v] * pages))


def _s5_discretise(a_re, a_im, log_dt):
    dt = jnp.exp(log_dt)
    mag = jnp.exp(a_re * dt)
    ang = a_im * dt
    abar_re = mag * jnp.cos(ang)
    abar_im = mag * jnp.sin(ang)
    den = a_re * a_re + a_im * a_im
    num_re = abar_re - 1.0
    zoh_re = (num_re * a_re + abar_im * a_im) / den
    zoh_im = (abar_im * a_re - num_re * a_im) / den
    return abar_re, abar_im, zoh_re, zoh_im


def _cmul(ar, ai, br, bi):
    return ar * br - ai * bi, ar * bi + ai * br


def _s5_tables_kernel(ar_r, ai_r, ld_r, br_r, bi_r, ar_t, ai_t, ld_t, cr_t, ci_t, ar_f, ai_f, ld_f,
                      kt_ref, wt_ref, et_ref, w0_ref, e0_ref, ap_ref, *, chunk, gsz, psz):
    ngrp = LANES // gsz
    pw = ngrp * psz
    abr, abi, zr, zi = _s5_discretise(ar_r[...], ai_r[...], ld_r[...])
    bbr, bbi = _cmul(zr, zi, br_r[...], bi_r[...])
    atr, ati, _, _ = _s5_discretise(ar_t[...], ai_t[...], ld_t[...])
    ctr, cti = cr_t[...], ci_t[...]
    afr, afi, _, _ = _s5_discretise(ar_f[...], ai_f[...], ld_f[...])

    def group_of(shape, axis, size):
        return _idiv(lax.broadcasted_iota(jnp.int32, shape, axis), size)

    same_kk = group_of((LANES, LANES), 0, gsz) == group_of((LANES, LANES), 1, gsz)
    same_wp = group_of((LANES, pw), 0, gsz) == group_of((LANES, pw), 1, psz)
    same_pe = group_of((pw, LANES), 0, psz) == group_of((pw, LANES), 1, gsz)
    rep = jnp.where(_imod(lax.broadcasted_iota(jnp.int32, (psz, pw), 1), psz)
                    == lax.broadcasted_iota(jnp.int32, (psz, pw), 0), 1.0, 0.0).astype(F32)

    def w_table(wr, wi):
        return jnp.concatenate([jnp.where(same_wp, _dot_f32(wr, rep), 0.0),
                                jnp.where(same_wp, _dot_f32(wi, rep), 0.0)], axis=1)

    def e_table(qr, qi):
        return jnp.concatenate([jnp.where(same_pe, jnp.tile(qr, (ngrp, 1)), 0.0),
                                jnp.where(same_pe, jnp.tile(-qi, (ngrp, 1)), 0.0)], axis=0)

    w0_ref[...] = w_table(bbr, bbi)
    e0_ref[...] = e_table(ctr, cti)
    ap_ref[0:1, :] = jnp.concatenate([afr, afi], axis=1)

    pr_r, pi_r = jnp.ones_like(abr), jnp.zeros_like(abr)
    pr_t, pi_t = jnp.ones_like(atr), jnp.zeros_like(atr)
    pr_f, pi_f = jnp.ones_like(afr), jnp.zeros_like(afr)
    for tau in range(chunk):
        wr, wi = _cmul(pr_r, pi_r, bbr, bbi)
        kt_ref[tau] = jnp.where(same_kk, _dot_f32(wr, ctr) - _dot_f32(wi, cti), 0.0).astype(kt_ref.dtype)
        wt_ref[tau] = w_table(wr, wi).astype(wt_ref.dtype)
        pr_r, pi_r = _cmul(pr_r, pi_r, abr, abi)
        pr_t, pi_t = _cmul(pr_t, pi_t, atr, ati)
        pr_f, pi_f = _cmul(pr_f, pi_f, afr, afi)
        qr, qi = _cmul(ctr, cti, pr_t, pi_t)
        et_ref[tau] = e_table(qr, qi).astype(et_ref.dtype)
    ap_ref[1:2, :] = jnp.concatenate([pr_f, pi_f], axis=1)


def s5_tables(a_re, a_im, log_dt, b_re, b_im, c_re, c_im):
    g, p = a_re.shape
    c = b_re.shape[2]
    assert LANES % c == 0 and (g * c) % LANES == 0
    nl = g * c // LANES
    ngrp = LANES // c
    pw = ngrp * p
    chunk = S5_CHUNK
    ld = jnp.broadcast_to(log_dt[:, None], (g, p))
    rows = lambda a: jnp.repeat(a, c, axis=0)
    lanes = lambda a: jnp.repeat(a.T, c, axis=1)
    flat = lambda a: a.reshape(1, g * p)
    args = [rows(a_re), rows(a_im), rows(ld),
            b_re.transpose(0, 2, 1).reshape(g * c, p), b_im.transpose(0, 2, 1).reshape(g * c, p),
            lanes(a_re), lanes(a_im), lanes(ld),
            c_re.transpose(2, 0, 1).reshape(p, g * c), c_im.transpose(2, 0, 1).reshape(p, g * c),
            flat(a_re), flat(a_im), flat(ld)]
    spec_r = pl.BlockSpec((LANES, p), lambda i: (i, 0))
    spec_t = pl.BlockSpec((p, LANES), lambda i: (0, i))
    spec_f = pl.BlockSpec((1, pw), lambda i: (0, i))
    return pl.pallas_call(
        functools.partial(_s5_tables_kernel, chunk=chunk, gsz=c, psz=p),
        grid=(nl,),
        in_specs=[spec_r] * 5 + [spec_t] * 5 + [spec_f] * 3,
        out_specs=[pl.BlockSpec((None, chunk, LANES, LANES), lambda i: (i, 0, 0, 0)),
                   pl.BlockSpec((None, chunk, LANES, 2 * pw), lambda i: (i, 0, 0, 0)),
                   pl.BlockSpec((None, chunk, 2 * pw, LANES), lambda i: (i, 0, 0, 0)),
                   pl.BlockSpec((None, LANES, 2 * pw), lambda i: (i, 0, 0)),
                   pl.BlockSpec((None, 2 * pw, LANES), lambda i: (i, 0, 0)),
                   pl.BlockSpec((None, 2, 2 * pw), lambda i: (i, 0, 0))],
        out_shape=[jax.ShapeDtypeStruct((nl, chunk, LANES, LANES), BF16),
                   jax.ShapeDtypeStruct((nl, chunk, LANES, 2 * pw), BF16),
                   jax.ShapeDtypeStruct((nl, chunk, 2 * pw, LANES), BF16),
                   jax.ShapeDtypeStruct((nl, LANES, 2 * pw), F32),
                   jax.ShapeDtypeStruct((nl, 2 * pw, LANES), F32),
                   jax.ShapeDtypeStruct((nl, 2, 2 * pw), F32)],
        compiler_params=_params("parallel"),
    )(*args)


def _s5_prompt_kernel(u_ref, d_ref, kt_ref, wt_ref, et_ref, ap_ref, z_ref, hre_ref, him_ref, s_sc,
                      *, chunk, bsz):
    m = u_ref.shape[0]
    rows = m // chunk
    nchunk = rows // bsz
    pw = ap_ref.shape[1] // 2
    u = [u_ref[pl.ds(i, rows, stride=chunk), :] for i in range(chunk)]
    ub = [x.astype(BF16) for x in u]
    s = _dot(ub[0], wt_ref[chunk - 1])
    for i in range(1, chunk):
        s = s + _dot(ub[i], wt_ref[chunk - 1 - i])
    s_sc[...] = s
    a_r = ap_ref[1:2, 0:pw]
    a_i = ap_ref[1:2, pw:2 * pw]
    def body(g8, carry):
        out = []
        for b in range(bsz):
            hr, hi = carry[2 * b], carry[2 * b + 1]
            base = pl.multiple_of(b * nchunk + g8 * SUBLANES, SUBLANES)
            tile = s_sc[pl.ds(base, SUBLANES), :]
            before_r, before_i = [], []
            for r in range(SUBLANES):
                before_r.append(hr)
                before_i.append(hi)
                nr = a_r * hr - a_i * hi + tile[r:r + 1, 0:pw]
                ni = a_r * hi + a_i * hr + tile[r:r + 1, pw:2 * pw]
                hr, hi = nr, ni
            s_sc[pl.ds(base, SUBLANES), 0:pw] = jnp.concatenate(before_r, axis=0)
            s_sc[pl.ds(base, SUBLANES), pw:2 * pw] = jnp.concatenate(before_i, axis=0)
            out += [hr, hi]
        return tuple(out)

    zero = jnp.zeros((1, pw), F32)
    last = lax.fori_loop(0, nchunk // SUBLANES, body, (zero,) * (2 * bsz))
    for b in range(bsz):
        hre_ref[b:b + 1, :] = last[2 * b]
        him_ref[b:b + 1, :] = last[2 * b + 1]
    h_in = s_sc[...].astype(BF16)
    d = d_ref[...]
    for i in range(chunk):
        y = _dot(h_in, et_ref[i])
        for j in range(i + 1):
            y = y + _dot(ub[j], kt_ref[i - j])
        y = y + d * u[i]
        z_ref[pl.ds(i, rows, stride=chunk), :] = _gelu_tanh(y)


def s5_prompt(u, d_skip, kt, wt, et, ap, bsz):
    m, width = u.shape
    nl = width // LANES
    chunk = kt.shape[1]
    pw2 = ap.shape[2]
    rows = m // chunk
    assert (m // bsz) % (chunk * SUBLANES) == 0
    col = pl.BlockSpec((m, LANES), lambda i: (0, i))
    tab = lambda a: pl.BlockSpec((None,) + a.shape[1:], lambda i: (i,) + (0,) * (a.ndim - 1))
    st = pl.BlockSpec((bsz, pw2 // 2), lambda i: (0, i))
    return pl.pallas_call(
        functools.partial(_s5_prompt_kernel, chunk=chunk, bsz=bsz),
        grid=(nl,),
        in_specs=[col, pl.BlockSpec((1, LANES), lambda i: (0, i)), tab(kt), tab(wt), tab(et), tab(ap)],
        out_specs=[col, st, st],
        out_shape=[jax.ShapeDtypeStruct((m, width), F32),
                   jax.ShapeDtypeStruct((bsz, nl * pw2 // 2), F32),
                   jax.ShapeDtypeStruct((bsz, nl * pw2 // 2), F32)],
        scratch_shapes=[pltpu.VMEM((rows, pw2), F32)],
        compiler_params=_params("parallel"),
    )(u, d_skip.reshape(1, width), kt, wt, et, ap)


def _s5_sample_kernel(u_ref, d_ref, hre_ref, him_ref, w0_ref, e0_ref, ap_ref, z_ref, ore_ref, oim_ref):
    pw = hre_ref.shape[1]
    u = u_ref[...]
    bu = _dot_f32(u, w0_ref[...])
    a_r = ap_ref[0:1, 0:pw]
    a_i = ap_ref[0:1, pw:2 * pw]
    hr, hi = hre_ref[...], him_ref[...]
    nr = a_r * hr - a_i * hi + bu[:, 0:pw]
    ni = a_r * hi + a_i * hr + bu[:, pw:2 * pw]
    y = _dot_f32(jnp.concatenate([nr, ni], axis=1), e0_ref[...]) + d_ref[...] * u
    z_ref[...] = _gelu_tanh(y)
    ore_ref[...] = nr
    oim_ref[...] = ni


def s5_sample(u, d_skip, h_re, h_im, w0, e0, ap):
    m, width = u.shape
    nl = width // LANES
    pw2 = ap.shape[2]
    col = pl.BlockSpec((m, LANES), lambda i: (0, i))
    tab = lambda a: pl.BlockSpec((None,) + a.shape[1:], lambda i: (i,) + (0,) * (a.ndim - 1))
    st = pl.BlockSpec((m, pw2 // 2), lambda i: (0, i))
    return pl.pallas_call(
        _s5_sample_kernel,
        grid=(nl,),
        in_specs=[col, pl.BlockSpec((1, LANES), lambda i: (0, i)), st, st, tab(w0), tab(e0), tab(ap)],
        out_specs=[col, st, st],
        out_shape=[jax.ShapeDtypeStruct((m, width), F32),
                   jax.ShapeDtypeStruct(h_re.shape, F32),
                   jax.ShapeDtypeStruct(h_im.shape, F32)],
        compiler_params=_params("parallel"),
    )(u, d_skip.reshape(1, width), h_re, h_im, w0, e0, ap)


def _layer(xp, xs, l, big_w, lw, kv, dims, prompt, st):
    cw, fw, sw, heads, dh, d_ff = dims
    m, d = xp.shape
    ms = xs.shape[0]
    depth = big_w["w_in_t"].shape[0]
    tm = min(1024, m)
    xn, xn_s = rmsnorm(xp, lw["norm_mix"], BF16), rmsnorm(xs, lw["norm_mix"], BF16)
    proj = lambda off, n, ep, dt, extras=(), stack=None, tn=min(1024, cw): matmul_ws(
        xn, xn_s, [(big_w["w_in_t"], off)], n, ep, list(extras), dt, tm // 2, tn, l, transposed=True,
        stack=stack)
    f0 = 3 * cw + 3 * fw
    hbc, hbc_s = proj(0, 3 * cw, _ep_plain, F32)
    q, q_s = proj(3 * cw, fw, _ep_plain, BF16)
    k_all = proj(3 * cw + fw, fw, _ep_plain, F32, stack=(kv and kv[0], l, depth))
    v_all = proj(3 * cw + 2 * fw, fw, _ep_plain, F32, stack=(kv and kv[1], l, depth))
    logf, logf_s = proj(f0, LANES, _ep_logf, F32, extras=[(lw["fox_bias"], None, "row", 0)], tn=LANES)
    u_c, u_c_s = proj(f0 + heads, sw, _ep_plain, F32)
    sg, sg_s = proj(f0 + heads + sw, 3 * d, _ep_sigmoid, F32)

    kt, wt, et, w0, e0, ap = lw["s5_tables"]
    bsz, t = prompt
    y_a, conv_new = conv_prompt(hbc, lw["conv_wt"], bsz, t)
    f, ft = forget_cumsum(logf, bsz, t)
    y_b = fox_prompt(q, k_all[0], v_all[0], l, f, ft, bsz, t, heads, dh)
    z, h_re, h_im = s5_prompt(u_c, lw["ssm_d"], kt, wt, et, ap, bsz)

    buf = st["conv"]
    y_a_s, u_a = conv_sample(hbc_s, buf[:, 0], buf[:, 1], lw["conv_wt"])
    conv_new_s = jnp.stack([buf[:, 1], u_a], axis=1)
    ps = st["cache_k"].shape[2]
    after = forget_after(st["page_table"], st["cache_logf"], l)
    db, npg = st["page_table"].shape
    y_b_s = fox_sample(st["page_table"], q_s.reshape(ms, heads, dh), k_all[1][l].reshape(ms, heads, dh),
                       v_all[1][l].reshape(ms, heads, dh),
                       jnp.tile(logf_s[:, :heads], (1, ps)).reshape(db, 1, ps * heads),
                       after.reshape(db, npg, 1, ps * heads), st["cache_k"], st["cache_v"], l)
    y_b_s = y_b_s.reshape(ms, fw)
    z_s, h_re_s, h_im_s = s5_sample(u_c_s, lw["ssm_d"], st["h_re"], st["h_im"], w0, e0, ap)

    glu = lambda zz: matmul(zz, [(big_w["w_glu_bf"], 0)], sw, _ep_glu, [(zz, "tile", 0)], BF16, tm, 1024, layer=l)
    merged, merged_s = gated_merge((y_a, y_b, glu(z)), (y_a_s, y_b_s, glu(z_s)), big_w["w_branch"], l,
                                   sg, sg_s, tm, 512)
    xp, xs = matmul_ws(merged, merged_s, [(big_w["w_out"], 0)], d, _ep_residual, [(xp, xs, "tile", 0)],
                       F32, tm, 512, l)
    xn2, xn2_s = rmsnorm(xp, lw["norm_ffn"], BF16), rmsnorm(xs, lw["norm_ffn"], BF16)
    tf = 256
    hid, hid_s = matmul_ws(xn2, xn2_s, [(big_w["w_up"], 0), (big_w["w_up"], d_ff // tf)], d_ff, _ep_swiglu,
                           [], BF16, 2 * tm, tf, l)
    down = lambda hh, xx: matmul(hh, [(big_w["w_down_bf"], 0)], d, _ep_residual, [(xx, "tile", 0)], F32,
                                 512, 512, layer=l)
    xp, xs = down(hid, xp), down(hid_s, xs)
    return (xp, xs, (k_all, v_all), (logf, conv_new, h_re, h_im), (logf_s, conv_new_s, h_re_s, h_im_s))


def kernel(x_prompt, x_sample, cache_k, cache_v, cache_logf, state_conv, state_ssm_re, state_ssm_im,
           page_table, norm_mix, w_in, conv_w, fox_bias, ssm_a_re, ssm_a_im, ssm_log_dt, ssm_b_re,
           ssm_b_im, ssm_c_re, ssm_c_im, ssm_d, w_glu, w_branch, w_out, norm_ffn, w_up, w_down,
           norm_final):
    bp, t, d = x_prompt.shape
    bd, td, _ = x_sample.shape
    depth = w_in.shape[0]
    cw = conv_w.shape[1]
    heads = fox_bias.shape[1]
    dh = cache_k.shape[4]
    fw = heads * dh
    g, p = ssm_a_re.shape[1:]
    sw = ssm_d.shape[1]
    d_ff = w_down.shape[1]
    assert td == 1
    dims = (cw, fw, sw, heads, dh, d_ff)

    big_w = {"w_in_t": jnp.swapaxes(w_in, 1, 2), "w_branch": w_branch, "w_out": w_out, "w_up": w_up,
             "w_glu_bf": w_glu.astype(BF16), "w_down_bf": w_down.astype(BF16)}
    xp = x_prompt.reshape(bp * t, d)
    xs = x_sample.reshape(bd * td, d)
    kv = None
    outs_p, outs_s = [], []
    for l in range(depth):
        lw = {
            "norm_mix": norm_mix[l], "norm_ffn": norm_ffn[l],
            "fox_bias": jnp.pad(fox_bias[l], (0, LANES - heads)).reshape(1, LANES),
            "conv_wt": conv_w[l].T,
            "ssm_d": ssm_d[l],
            "s5_tables": s5_tables(ssm_a_re[l], ssm_a_im[l], ssm_log_dt[l], ssm_b_re[l], ssm_b_im[l],
                                   ssm_c_re[l], ssm_c_im[l]),
        }
        st = {"conv": state_conv[l], "h_re": state_ssm_re[l].reshape(bd, g * p),
              "h_im": state_ssm_im[l].reshape(bd, g * p), "page_table": page_table,
              "cache_k": cache_k, "cache_v": cache_v, "cache_logf": cache_logf}
        xp, xs, kv, (lfp, cvp, hrp, hip), (lfs, cvs, hrs, his) = _layer(xp, xs, l, big_w, lw, kv, dims, (bp, t), st)
        outs_p.append((lfp[:, :heads].reshape(bp, t, heads), cvp, hrp.reshape(bp, g, p), hip.reshape(bp, g, p)))
        outs_s.append((lfs[:, :heads].reshape(bd, td, heads), cvs, hrs.reshape(bd, g, p), his.reshape(bd, g, p)))
    stack = lambda outs: [jnp.stack(a, axis=0) for a in zip(*outs)]
    y_prompt = rmsnorm(xp, norm_final, F32).reshape(bp, t, d)
    y_sample = rmsnorm(xs, norm_final, F32).reshape(bd, td, d)
    kv5 = lambda a, b, n: a.reshape(depth, b, n, heads, dh)
    (k_p, k_s), (v_p, v_s) = kv
    return (y_prompt, y_sample, kv5(k_p, bp, t), kv5(v_p, bp, t), *stack(outs_p),
            kv5(k_s, bd, td), kv5(v_s, bd, td), *stack(outs_s))
```

```python
import functools
import math

import jax
import jax.numpy as jnp
from jax import lax
from jax.experimental import pallas as pl
from jax.experimental.pallas import tpu as pltpu

F32 = jnp.float32
BF16 = jnp.bfloat16
HIGHEST = lax.Precision.HIGHEST

RMS_EPS = 1e-6
LOG2E = math.log2(math.e)
LANES = 128
SUBLANES = 8
VMEM_LIMIT = 60 * 1024 * 1024
S5_CHUNK = 8
NEG_BIG = -1e30
FOX_ROW_CHUNK = 32
ATTN_PAGES_PER_STEP = 8
AFTER_PAGES_PER_STEP = 16


def _params(*sem):
    return pltpu.CompilerParams(dimension_semantics=sem, vmem_limit_bytes=VMEM_LIMIT)


def _sigmoid(x):
    return 1.0 / (1.0 + jnp.exp(-x))


def _log_sigmoid(x):
    t = -x
    return -(jnp.maximum(t, 0.0) + jnp.log1p(jnp.exp(-jnp.abs(t))))


def _gelu_tanh(x):
    c = math.sqrt(2.0 / math.pi)
    return x * (0.5 * (1.0 + jnp.tanh(c * (x + 0.044715 * (x * x * x)))))


def _idiv(x, n):
    assert n & (n - 1) == 0
    return x >> (n.bit_length() - 1)


def _imod(x, n):
    assert n & (n - 1) == 0
    return x & (n - 1)


def _dot(a, b):
    return jnp.dot(a, b, preferred_element_type=F32)


def _dot_f32(a, b):
    return jnp.dot(a, b, preferred_element_type=F32, precision=HIGHEST)


def _rmsnorm_kernel(x_ref, g_ref, o_ref):
    x = x_ref[...]
    ms = jnp.mean(x * x, axis=-1, keepdims=True)
    o_ref[...] = (x * lax.rsqrt(ms + RMS_EPS) * g_ref[...]).astype(o_ref.dtype)


def rmsnorm(x, g, out_dtype):
    m, d = x.shape
    tm = min(512, m)
    return pl.pallas_call(
        _rmsnorm_kernel,
        grid=(m // tm,),
        in_specs=[pl.BlockSpec((tm, d), lambda i: (i, 0)),
                  pl.BlockSpec((1, d), lambda i: (0, 0))],
        out_specs=pl.BlockSpec((tm, d), lambda i: (i, 0)),
        out_shape=jax.ShapeDtypeStruct((m, d), out_dtype),
        compiler_params=_params("parallel"),
    )(x, g.reshape(1, d))


def _mm_kernel(*refs, n_w, epilogue):
    x_ref = refs[0]
    w_refs = refs[1:1 + n_w]
    e_refs = refs[1 + n_w:-1]
    o_ref = refs[-1]
    x = x_ref[...]
    if x.dtype != BF16:
        x = x.astype(BF16)
    accs = [_dot(x, w[...]) for w in w_refs]
    o_ref[...] = epilogue(accs, [e[...] for e in e_refs]).astype(o_ref.dtype)


def matmul(x, ws, n, epilogue, extras, out_dtype, tm, tn, layer=None):
    m, k = x.shape
    tm = min(tm, m)
    tn = min(tn, n)
    assert m % tm == 0 and n % tn == 0
    in_specs = [pl.BlockSpec((tm, k), lambda i, j: (i, 0))]
    args = [x]
    for w, c0 in ws:
        if w.ndim == 3:
            in_specs.append(pl.BlockSpec((None, k, tn), lambda i, j, c0=c0: (layer, 0, c0 + j)))
        else:
            in_specs.append(pl.BlockSpec((k, tn), lambda i, j, c0=c0: (0, c0 + j)))
        args.append(w)
    for a, kind, c0 in extras:
        if kind == "tile":
            in_specs.append(pl.BlockSpec((tm, tn), lambda i, j, c0=c0: (i, c0 + j)))
        else:
            in_specs.append(pl.BlockSpec((1, tn), lambda i, j, c0=c0: (0, c0 + j)))
        args.append(a)
    return pl.pallas_call(
        functools.partial(_mm_kernel, n_w=len(ws), epilogue=epilogue),
        grid=(m // tm, n // tn),
        in_specs=in_specs,
        out_specs=pl.BlockSpec((tm, tn), lambda i, j: (i, j)),
        out_shape=jax.ShapeDtypeStruct((m, n), out_dtype),
        compiler_params=_params("parallel", "parallel"),
    )(*args)


def _mmw_kernel(*refs, n_w, kinds, n_in, transposed, epilogue):
    x_ref = refs[0]
    w_refs = refs[1:1 + n_w]
    e_refs = refs[1 + n_w:1 + n_w + len(kinds)]
    n_side = sum(kind == "tile" for kind in kinds)
    xs_ref = refs[1 + n_w + len(kinds)]
    es_refs = list(refs[2 + n_w + len(kinds):2 + n_w + len(kinds) + n_side])
    o_ref, os_ref = refs[n_in], refs[n_in + 1]
    wb_refs = refs[-n_w:]

    def dots(x):
        if transposed:
            return [lax.dot_general(x, wb[...], (((1,), (1,)), ((), ())), preferred_element_type=F32)
                    for wb in wb_refs]
        return [_dot(x, wb[...]) for wb in wb_refs]

    @pl.when(pl.program_id(1) == 0)
    def _():
        for w, wb in zip(w_refs, wb_refs):
            wb[...] = (w[0] if transposed else w[...]).astype(BF16)
        extras_s = [es_refs.pop(0)[...] if kind == "tile" else e[...] for kind, e in zip(kinds, e_refs)]
        os_ref[...] = epilogue(dots(xs_ref[...]), extras_s).astype(os_ref.dtype)

    o_ref[...] = epilogue(dots(x_ref[...]), [e[...] for e in e_refs]).astype(o_ref.dtype)


def matmul_ws(x, xs, ws, n, epilogue, extras, out_dtype, tm, tn, layer, transposed=False, stack=None,
              w_single_buffer=False):
    m, k = x.shape
    ms = xs.shape[0]
    tm = min(tm, m)
    tn = min(tn, n)
    assert m % tm == 0 and n % tn == 0
    in_specs = [pl.BlockSpec((tm, k), lambda j, i: (i, 0))]
    args = [x]
    mode = pl.Buffered(1) if w_single_buffer else None
    for w, off in ws:
        if transposed:
            in_specs.append(pl.BlockSpec((pl.Element(1), pl.Element(tn), pl.Element(k)),
                                         lambda j, i, off=off: (layer, pl.multiple_of(off + j * tn, SUBLANES), 0),
                                         pipeline_mode=mode))
        else:
            in_specs.append(pl.BlockSpec((None, k, tn), lambda j, i, off=off: (layer, 0, off + j),
                                         pipeline_mode=mode))
        args.append(w)
    for a, _, kind, c0 in extras:
        if kind == "tile":
            in_specs.append(pl.BlockSpec((tm, tn), lambda j, i, c0=c0: (i, c0 + j)))
        else:
            in_specs.append(pl.BlockSpec((1, tn), lambda j, i, c0=c0: (0, c0 + j)))
        args.append(a)
    in_specs.append(pl.BlockSpec((ms, k), lambda j, i: (0, 0)))
    args.append(xs)
    for _, a_s, kind, c0 in extras:
        if kind == "tile":
            in_specs.append(pl.BlockSpec((ms, tn), lambda j, i, c0=c0: (0, c0 + j)))
            args.append(a_s)
    n_in = len(args)
    aliases = {}
    if stack is None:
        out_specs = [pl.BlockSpec((tm, tn), lambda j, i: (i, j)), pl.BlockSpec((ms, tn), lambda j, i: (0, j))]
        out_shape = [jax.ShapeDtypeStruct((m, n), out_dtype), jax.ShapeDtypeStruct((ms, n), out_dtype)]
    else:
        bufs, slab, depth = stack
        out_specs = [pl.BlockSpec((None, tm, tn), lambda j, i: (slab, i, j)),
                     pl.BlockSpec((None, ms, tn), lambda j, i: (slab, 0, j))]
        out_shape = [jax.ShapeDtypeStruct((depth, m, n), out_dtype),
                     jax.ShapeDtypeStruct((depth, ms, n), out_dtype)]
        if bufs is not None:
            in_specs += [pl.BlockSpec(memory_space=pl.ANY)] * 2
            args += list(bufs)
            aliases = {n_in: 0, n_in + 1: 1}
    return pl.pallas_call(
        functools.partial(_mmw_kernel, n_w=len(ws), kinds=tuple(e[2] for e in extras), n_in=len(args),
                          transposed=transposed, epilogue=epilogue),
        grid=(n // tn, m // tm),
        in_specs=in_specs,
        out_specs=out_specs,
        out_shape=out_shape,
        scratch_shapes=[pltpu.VMEM((tn, k) if transposed else (k, tn), BF16) for _ in ws],
        input_output_aliases=aliases,
        compiler_params=_params("parallel", "arbitrary"),
    )(*args)


def _ep_plain(accs, extras):
    return accs[0]


def _ep_sigmoid(accs, extras):
    return _sigmoid(accs[0])


def _ep_logf(accs, extras):
    return _log_sigmoid(accs[0] + extras[0])


def _ep_residual(accs, extras):
    return extras[0] + accs[0]


def _ep_glu(accs, extras):
    return extras[0] * _sigmoid(accs[0])


def _ep_swiglu(accs, extras):
    g = accs[0]
    return (g * _sigmoid(g)) * accs[1]


def _merge_kernel(w_ref, *refs):
    main, side, (o_ref, os_ref, wb_ref) = refs[0:6], refs[6:12], refs[12:]

    def merge(ya_ref, yb_ref, yc_ref, ga_ref, gb_ref, gc_ref, out_ref):
        ka = ya_ref.shape[1]
        kb = ka + yb_ref.shape[1]
        a = _dot(ya_ref[...], wb_ref[0:ka, :])
        b = _dot(yb_ref[...], wb_ref[ka:kb, :])
        c = _dot(yc_ref[...], wb_ref[kb:, :])
        out_ref[...] = (ga_ref[...] * a + gb_ref[...] * b + gc_ref[...] * c).astype(out_ref.dtype)

    @pl.when(pl.program_id(1) == 0)
    def _():
        wb_ref[...] = w_ref[...].astype(BF16)
        merge(*side, os_ref)

    merge(*main, o_ref)


def gated_merge(ys, ys_side, w_branch, layer, sg, sg_side, tm, tn):
    m = ys[0].shape[0]
    ms = ys_side[0].shape[0]
    _, mix, d = w_branch.shape
    tm = min(tm, m)
    tn = min(tn, d)
    nj = d // tn
    row = lambda a, r, s: pl.BlockSpec((r, a.shape[1]), lambda j, i: (i * s, 0))
    gate = lambda c, r, s: pl.BlockSpec((r, tn), lambda j, i: (i * s, c * nj + j))
    group = lambda y, r, s: [row(a, r, s) for a in y] + [gate(c, r, s) for c in range(3)]
    return pl.pallas_call(
        _merge_kernel,
        grid=(nj, m // tm),
        in_specs=([pl.BlockSpec((None, mix, tn), lambda j, i: (layer, 0, j))]
                  + group(ys, tm, 1) + group(ys_side, ms, 0)),
        out_specs=[pl.BlockSpec((tm, tn), lambda j, i: (i, j)), pl.BlockSpec((ms, tn), lambda j, i: (0, j))],
        out_shape=[jax.ShapeDtypeStruct((m, d), BF16), jax.ShapeDtypeStruct((ms, d), BF16)],
        scratch_shapes=[pltpu.VMEM((mix, tn), BF16)],
        compiler_params=_params("parallel", "arbitrary"),
    )(w_branch, *ys, sg, sg, sg, *ys_side, sg_side, sg_side, sg_side)


def _conv_prompt_kernel(h_ref, b_ref, c_ref, w_ref, y_ref, st_ref):
    u = c_ref[...] * h_ref[...]
    t = u.shape[0]
    row = lax.broadcasted_iota(jnp.int32, u.shape, 0)
    u1 = jnp.where(row >= 1, pltpu.roll(u, 1, 0), 0.0)
    u2 = jnp.where(row >= 2, pltpu.roll(u, 2, 0), 0.0)
    w = w_ref[...]
    conv = u2 * w[0:1, :] + u1 * w[1:2, :] + u * w[2:3, :]
    y_ref[...] = (b_ref[...] * conv).astype(y_ref.dtype)
    st_ref[...] = u[t - 2:t, :]


def conv_prompt(hbc, wt, bsz, t):
    cw = wt.shape[1]
    tc = min(256, cw)
    nc = cw // tc
    hbc3 = hbc.reshape(bsz, t, 3 * cw)
    spec = lambda off: pl.BlockSpec((None, t, tc), lambda b, c, off=off: (b, 0, off * nc + c))
    y, st = pl.pallas_call(
        _conv_prompt_kernel,
        grid=(bsz, nc),
        in_specs=[spec(0), spec(1), spec(2), pl.BlockSpec((3, tc), lambda b, c: (0, c))],
        out_specs=[pl.BlockSpec((None, t, tc), lambda b, c: (b, 0, c)),
                   pl.BlockSpec((None, 2, tc), lambda b, c: (b, 0, c))],
        out_shape=[jax.ShapeDtypeStruct((bsz, t, cw), BF16),
                   jax.ShapeDtypeStruct((bsz, 2, cw), F32)],
        compiler_params=_params("parallel", "parallel"),
    )(hbc3, hbc3, hbc3, wt)
    return y.reshape(bsz * t, cw), st


def _conv_sample_kernel(h_ref, b_ref, c_ref, buf0_ref, buf1_ref, w_ref, y_ref, u_ref):
    u = c_ref[...] * h_ref[...]
    w = w_ref[...]
    conv = buf0_ref[...] * w[0:1, :] + buf1_ref[...] * w[1:2, :] + u * w[2:3, :]
    y_ref[...] = (b_ref[...] * conv).astype(y_ref.dtype)
    u_ref[...] = u


def conv_sample(hbc, buf0, buf1, wt):
    m, cw = buf0.shape
    spec = lambda off: pl.BlockSpec((m, cw), lambda i, off=off: (0, off))
    return pl.pallas_call(
        _conv_sample_kernel,
        grid=(1,),
        in_specs=[spec(0), spec(1), spec(2), spec(0), spec(0), pl.BlockSpec((3, cw), lambda i: (0, 0))],
        out_specs=[spec(0), spec(0)],
        out_shape=[jax.ShapeDtypeStruct((m, cw), BF16), jax.ShapeDtypeStruct((m, cw), F32)],
        compiler_params=_params("arbitrary"),
    )(hbc, hbc, hbc, buf0, buf1, wt)


def _fcum_kernel(lf_ref, f_ref, ft_ref):
    x = lf_ref[...]
    t = x.shape[0]
    row = lax.broadcasted_iota(jnp.int32, x.shape, 0)
    s = 1
    while s < t:
        x = x + jnp.where(row >= s, pltpu.roll(x, s, 0), 0.0)
        s *= 2
    f_ref[...] = x
    ft_ref[...] = x.T


def forget_cumsum(logf, bsz, t):
    lf3 = logf.reshape(bsz, t, LANES)
    return pl.pallas_call(
        _fcum_kernel,
        grid=(bsz,),
        in_specs=[pl.BlockSpec((None, t, LANES), lambda b: (b, 0, 0))],
        out_specs=[pl.BlockSpec((None, t, LANES), lambda b: (b, 0, 0)),
                   pl.BlockSpec((None, LANES, t), lambda b: (b, 0, 0))],
        out_shape=[jax.ShapeDtypeStruct((bsz, t, LANES), F32),
                   jax.ShapeDtypeStruct((bsz, LANES, t), F32)],
        compiler_params=_params("parallel"),
    )(lf3)


def _fox_prompt_kernel(q_ref, k_ref, v_ref, f_ref, ft_ref, o_ref, m_sc, acc_sc, fq_sc, s_sc, p_sc, pm_sc,
                       *, heads, dh, scale, tq, tk):
    qi = pl.program_id(1)
    ki = pl.program_id(2)
    rc = FOX_ROW_CHUNK
    nlb = tk // LANES

    @pl.when(ki == 0)
    def _():
        m_sc[...] = jnp.full(m_sc.shape, NEG_BIG, F32)
        acc_sc[...] = jnp.zeros(acc_sc.shape, F32)
        fq = f_ref[...] * LOG2E
        for h in range(heads):
            fq_sc[h] = jnp.broadcast_to(fq[:, h:h + 1], (tq, LANES))

    def block(diagonal):
        fk = ft_ref[...] * LOG2E
        ones = jnp.ones((tk, LANES), BF16)
        ahead = (lax.broadcasted_iota(jnp.int32, (rc, tk), 1) - lax.broadcasted_iota(jnp.int32, (rc, tk), 0))
        for h in range(heads):
            sl = slice(h * dh, (h + 1) * dh)
            buf = h % 2
            kh = k_ref[:, sl].astype(BF16)
            vh = jnp.concatenate([v_ref[:, sl].astype(BF16), ones], axis=1)
            s_sc[buf] = lax.dot_general(q_ref[:, sl], kh, (((1,), (1,)), ((), ())),
                                        preferred_element_type=F32)
            fk_h = fk[h:h + 1, :]
            for r0 in range(0, tq, rc):
                rows = slice(r0, r0 + rc)
                x = s_sc[buf, rows, :] * (scale * LOG2E) + jnp.tile(fq_sc[h, rows, :], (1, nlb)) - fk_h
                if diagonal:
                    x = jnp.where(ahead <= r0, x, NEG_BIG)
                s_sc[buf, rows, :] = x
                pm = x[:, 0:LANES]
                for j in range(1, nlb):
                    pm = jnp.maximum(pm, x[:, j * LANES:(j + 1) * LANES])
                pm_sc[buf, rows, :] = pm
            m_prev = m_sc[h]
            m_new = jnp.maximum(m_prev, jnp.max(pm_sc[buf], axis=1, keepdims=True))
            m_sc[h] = m_new
            pm_sc[buf] = m_new
            alpha = jnp.exp2(m_prev - m_new)
            for r0 in range(0, tq, rc):
                rows = slice(r0, r0 + rc)
                x = s_sc[buf, rows, :] - jnp.tile(pm_sc[buf, rows, :], (1, nlb))
                p_sc[buf, rows, :] = jnp.exp2(x).astype(BF16)
            acc_sc[h] = jnp.tile(alpha, (1, 2)) * acc_sc[h] + _dot(p_sc[buf], vh)

    @pl.when(ki < qi)
    def _():
        block(False)

    @pl.when(ki == qi)
    def _():
        block(True)
        for h in range(heads):
            acc = acc_sc[h]
            o_ref[:, h * dh:(h + 1) * dh] = (acc[:, 0:dh] / acc[:, dh:2 * dh]).astype(o_ref.dtype)


def fox_prompt(q, k, v, layer, f, ft, bsz, t, heads, dh):
    assert dh == LANES
    tq = tk = min(512, t)
    nq = t // tq
    width = heads * dh
    hp = min(LANES, -(-heads // SUBLANES) * SUBLANES)
    kernel = functools.partial(_fox_prompt_kernel, heads=heads, dh=dh, scale=dh ** -0.5, tq=tq, tk=tk)
    kv = pl.BlockSpec((None, tk, width), lambda b, i, j: (layer, b * nq + jnp.minimum(i, j), 0))
    return pl.pallas_call(
        kernel,
        grid=(bsz, nq, nq),
        in_specs=[
            pl.BlockSpec((tq, width), lambda b, i, j: (b * nq + i, 0)),
            kv, kv,
            pl.BlockSpec((None, tq, LANES), lambda b, i, j: (b, i, 0)),
            pl.BlockSpec((None, hp, tk), lambda b, i, j: (b, 0, jnp.minimum(i, j))),
        ],
        out_specs=pl.BlockSpec((tq, width), lambda b, i, j: (b * nq + i, 0)),
        out_shape=jax.ShapeDtypeStruct((bsz * t, width), BF16),
        scratch_shapes=[pltpu.VMEM((heads, tq, LANES), F32),
                        pltpu.VMEM((heads, tq, 2 * dh), F32),
                        pltpu.VMEM((heads, tq, LANES), F32),
                        pltpu.VMEM((2, tq, tk), F32),
                        pltpu.VMEM((2, tq, tk), BF16),
                        pltpu.VMEM((2, tq, LANES), F32)],
        compiler_params=_params("parallel", "parallel", "arbitrary"),
    )(q, k, v, f, ft)


def _after_kernel(pt_ref, *refs, pages):
    lf_refs = refs[:pages]
    o_ref, carry_sc = refs[pages], refs[pages + 1]

    @pl.when(pl.program_id(1) == 0)
    def _():
        carry_sc[...] = jnp.zeros(carry_sc.shape, F32)

    ps = lf_refs[0].shape[0]
    r = lax.broadcasted_iota(jnp.int32, (ps, ps), 0)
    c = lax.broadcasted_iota(jnp.int32, (ps, ps), 1)
    later = jnp.where(c > r, 1.0, 0.0).astype(F32)
    carry = carry_sc[...]
    for s in range(pages):
        lf = lf_refs[s][...]
        o_ref[pages - 1 - s] = _dot_f32(later, lf) + carry
        carry = carry + jnp.sum(lf, axis=0, keepdims=True)
    carry_sc[...] = carry


def forget_after(page_table, cache_logf, layer):
    db, npg = page_table.shape
    _, _, ps, heads = cache_logf.shape
    pages = math.gcd(npg, AFTER_PAGES_PER_STEP)
    nsteps = npg // pages
    lf_spec = lambda s: pl.BlockSpec((None, None, ps, heads),
                                     lambda b, p, pt: (layer, pt[b, npg - 1 - (p * pages + s)], 0, 0))
    grid_spec = pltpu.PrefetchScalarGridSpec(
        num_scalar_prefetch=1,
        grid=(db, nsteps),
        in_specs=[lf_spec(s) for s in range(pages)],
        out_specs=pl.BlockSpec((None, pages, ps, heads), lambda b, p, pt: (b, nsteps - 1 - p, 0, 0)),
        scratch_shapes=[pltpu.VMEM((1, heads), F32)],
    )
    return pl.pallas_call(
        functools.partial(_after_kernel, pages=pages),
        grid_spec=grid_spec,
        out_shape=jax.ShapeDtypeStruct((db, npg, ps, heads), F32),
        compiler_params=_params("parallel", "arbitrary"),
    )(page_table, *([cache_logf] * pages))


def _lane_class_reduce(x, op, period):
    s = period
    while s < LANES:
        x = op(x, pltpu.roll(x, s, 1))
        s *= 2
    return x


def _lane_to_col(x, heads):
    sub = lax.broadcasted_iota(jnp.int32, (heads, LANES), 0)
    lane = lax.broadcasted_iota(jnp.int32, (heads, LANES), 1)
    return jnp.sum(jnp.where(lane == sub, jnp.broadcast_to(x, (heads, LANES)), 0.0), axis=1, keepdims=True)


def _fox_sample_kernel(pt_ref, q_ref, kn_ref, vn_ref, fn_ref, aft_ref, *refs, scale, heads, pages):
    k_refs = refs[:pages]
    v_refs = refs[pages:2 * pages]
    o_ref, m_sc, l_sc, acc_sc = refs[2 * pages:]
    p = pl.program_id(1)
    ps, _, dh = k_refs[0].shape
    npg = ps * heads
    n = pages * npg
    nb = n // LANES

    @pl.when(p == 0)
    def _():
        m_sc[...] = jnp.full(m_sc.shape, NEG_BIG, F32)
        l_sc[...] = jnp.zeros(l_sc.shape, F32)
        acc_sc[...] = jnp.zeros(acc_sc.shape, F32)

    q = q_ref[...]
    k2 = jnp.concatenate([r[...].reshape(npg, dh).astype(BF16) for r in k_refs], axis=0)
    v2 = jnp.concatenate([r[...].reshape(npg, dh).astype(BF16) for r in v_refs], axis=0)
    st = lax.dot_general(q, k2, (((1,), (1,)), ((), ())), preferred_element_type=F32)
    sub = lax.broadcasted_iota(jnp.int32, (heads, n), 0)
    lane = lax.broadcasted_iota(jnp.int32, (heads, n), 1)
    own = _imod(lane, heads) == sub
    s = jnp.sum(jnp.where(own, st, 0.0), axis=0, keepdims=True) * scale
    bias = jnp.concatenate([aft_ref[j] for j in range(pages)], axis=1)
    s = s + jnp.tile(fn_ref[...], (1, pages)) + bias

    m_page = s[:, 0:LANES]
    for j in range(1, nb):
        m_page = jnp.maximum(m_page, s[:, j * LANES:(j + 1) * LANES])
    m_page = _lane_class_reduce(m_page, jnp.maximum, heads)
    m_prev = m_sc[...]
    m_new = jnp.maximum(m_prev, m_page)
    alpha = jnp.exp(m_prev - m_new)
    pr = jnp.exp(s - jnp.tile(m_new, (1, nb)))
    l_page = pr[:, 0:LANES]
    for j in range(1, nb):
        l_page = l_page + pr[:, j * LANES:(j + 1) * LANES]
    l_sc[...] = alpha * l_sc[...] + l_page
    m_sc[...] = m_new
    pm = jnp.where(own, jnp.broadcast_to(pr, (heads, n)), 0.0).astype(BF16)
    acc_sc[...] = acc_sc[...] * _lane_to_col(alpha, heads) + _dot(pm, v2)

    @pl.when(p == pl.num_programs(1) - 1)
    def _():
        s_new = jnp.sum(q.astype(F32) * kn_ref[...], axis=1, keepdims=True) * scale
        m_col = _lane_to_col(m_sc[...], heads)
        l_col = _lane_to_col(_lane_class_reduce(l_sc[...], jnp.add, heads), heads)
        m_fin = jnp.maximum(m_col, s_new)
        a_col = jnp.exp(m_col - m_fin)
        p_new = jnp.exp(s_new - m_fin)
        num = acc_sc[...] * a_col + p_new * vn_ref[...]
        o_ref[...] = (num / (l_col * a_col + p_new)).astype(o_ref.dtype)


def fox_sample(page_table, q, kn, vn, fn_row, after_row, cache_k, cache_v, layer):
    db, npg = page_table.shape
    _, _, ps, heads, dh = cache_k.shape
    n = ps * heads
    assert LANES % heads == 0 and n % LANES == 0
    pages = math.gcd(npg, ATTN_PAGES_PER_STEP)
    tok = pl.BlockSpec((None, heads, dh), lambda b, p, pt: (b, 0, 0))
    page = lambda s: pl.BlockSpec((None, None, ps, heads, dh),
                                  lambda b, p, pt: (layer, pt[b, p * pages + s], 0, 0, 0))
    grid_spec = pltpu.PrefetchScalarGridSpec(
        num_scalar_prefetch=1,
        grid=(db, npg // pages),
        in_specs=[tok, tok, tok,
                  pl.BlockSpec((None, 1, n), lambda b, p, pt: (b, 0, 0)),
                  pl.BlockSpec((None, pages, 1, n), lambda b, p, pt: (b, p, 0, 0))]
                 + [page(s) for s in range(pages)] * 2,
        out_specs=tok,
        scratch_shapes=[pltpu.VMEM((1, LANES), F32), pltpu.VMEM((1, LANES), F32),
                        pltpu.VMEM((heads, dh), F32)],
    )
    return pl.pallas_call(
        functools.partial(_fox_sample_kernel, scale=dh ** -0.5, heads=heads, pages=pages),
        grid_spec=grid_spec,
        out_shape=jax.ShapeDtypeStruct((db, heads, dh), BF16),
        compiler_params=_params("parallel", "arbitrary"),
    )(page_table, q, kn, vn, fn_row, after_row, *([cache_k] * pages), *([cache_v] * pages))


def _s5_discretise(a_re, a_im, log_dt):
    dt = jnp.exp(log_dt)
    mag = jnp.exp(a_re * dt)
    ang = a_im * dt
    abar_re = mag * jnp.cos(ang)
    abar_im = mag * jnp.sin(ang)
    den = a_re * a_re + a_im * a_im
    num_re = abar_re - 1.0
    zoh_re = (num_re * a_re + abar_im * a_im) / den
    zoh_im = (abar_im * a_re - num_re * a_im) / den
    return abar_re, abar_im, zoh_re, zoh_im


def _cmul(ar, ai, br, bi):
    return ar * br - ai * bi, ar * bi + ai * br


def _s5_tables_kernel(ar_r, ai_r, ld_r, br_r, bi_r, ar_t, ai_t, ld_t, cr_t, ci_t, ar_f, ai_f, ld_f,
                      kt_ref, wt_ref, et_ref, w0_ref, e0_ref, ap_ref, *, chunk, gsz, psz):
    ngrp = LANES // gsz
    pw = ngrp * psz
    abr, abi, zr, zi = _s5_discretise(ar_r[...], ai_r[...], ld_r[...])
    bbr, bbi = _cmul(zr, zi, br_r[...], bi_r[...])
    atr, ati, _, _ = _s5_discretise(ar_t[...], ai_t[...], ld_t[...])
    ctr, cti = cr_t[...], ci_t[...]
    afr, afi, _, _ = _s5_discretise(ar_f[...], ai_f[...], ld_f[...])

    def group_of(shape, axis, size):
        return _idiv(lax.broadcasted_iota(jnp.int32, shape, axis), size)

    same_kk = group_of((LANES, LANES), 0, gsz) == group_of((LANES, LANES), 1, gsz)
    same_wp = group_of((LANES, pw), 0, gsz) == group_of((LANES, pw), 1, psz)
    same_pe = group_of((pw, LANES), 0, psz) == group_of((pw, LANES), 1, gsz)
    rep = jnp.where(_imod(lax.broadcasted_iota(jnp.int32, (psz, pw), 1), psz)
                    == lax.broadcasted_iota(jnp.int32, (psz, pw), 0), 1.0, 0.0).astype(F32)

    def w_table(wr, wi):
        return jnp.concatenate([jnp.where(same_wp, _dot_f32(wr, rep), 0.0),
                                jnp.where(same_wp, _dot_f32(wi, rep), 0.0)], axis=1)

    def e_table(qr, qi):
        return jnp.concatenate([jnp.where(same_pe, jnp.tile(qr, (ngrp, 1)), 0.0),
                                jnp.where(same_pe, jnp.tile(-qi, (ngrp, 1)), 0.0)], axis=0)

    w0_ref[...] = w_table(bbr, bbi)
    e0_ref[...] = e_table(ctr, cti)
    ap_ref[0:1, :] = jnp.concatenate([afr, afi], axis=1)

    pr_r, pi_r = jnp.ones_like(abr), jnp.zeros_like(abr)
    pr_t, pi_t = jnp.ones_like(atr), jnp.zeros_like(atr)
    pr_f, pi_f = jnp.ones_like(afr), jnp.zeros_like(afr)
    for tau in range(chunk):
        wr, wi = _cmul(pr_r, pi_r, bbr, bbi)
        kt_ref[tau] = jnp.where(same_kk, _dot_f32(wr, ctr) - _dot_f32(wi, cti), 0.0).astype(kt_ref.dtype)
        wt_ref[tau] = w_table(wr, wi).astype(wt_ref.dtype)
        pr_r, pi_r = _cmul(pr_r, pi_r, abr, abi)
        pr_t, pi_t = _cmul(pr_t, pi_t, atr, ati)
        pr_f, pi_f = _cmul(pr_f, pi_f, afr, afi)
        qr, qi = _cmul(ctr, cti, pr_t, pi_t)
        et_ref[tau] = e_table(qr, qi).astype(et_ref.dtype)
    ap_ref[1:2, :] = jnp.concatenate([pr_f, pi_f], axis=1)


def s5_tables(a_re, a_im, log_dt, b_re, b_im, c_re, c_im):
    g, p = a_re.shape
    c = b_re.shape[2]
    assert LANES % c == 0 and (g * c) % LANES == 0
    nl = g * c // LANES
    ngrp = LANES // c
    pw = ngrp * p
    chunk = S5_CHUNK
    ld = jnp.broadcast_to(log_dt[:, None], (g, p))
    rows = lambda a: jnp.repeat(a, c, axis=0)
    lanes = lambda a: jnp.repeat(a.T, c, axis=1)
    flat = lambda a: a.reshape(1, g * p)
    args = [rows(a_re), rows(a_im), rows(ld),
            b_re.transpose(0, 2, 1).reshape(g * c, p), b_im.transpose(0, 2, 1).reshape(g * c, p),
            lanes(a_re), lanes(a_im), lanes(ld),
            c_re.transpose(2, 0, 1).reshape(p, g * c), c_im.transpose(2, 0, 1).reshape(p, g * c),
            flat(a_re), flat(a_im), flat(ld)]
    spec_r = pl.BlockSpec((LANES, p), lambda i: (i, 0))
    spec_t = pl.BlockSpec((p, LANES), lambda i: (0, i))
    spec_f = pl.BlockSpec((1, pw), lambda i: (0, i))
    return pl.pallas_call(
        functools.partial(_s5_tables_kernel, chunk=chunk, gsz=c, psz=p),
        grid=(nl,),
        in_specs=[spec_r] * 5 + [spec_t] * 5 + [spec_f] * 3,
        out_specs=[pl.BlockSpec((None, chunk, LANES, LANES), lambda i: (i, 0, 0, 0)),
                   pl.BlockSpec((None, chunk, LANES, 2 * pw), lambda i: (i, 0, 0, 0)),
                   pl.BlockSpec((None, chunk, 2 * pw, LANES), lambda i: (i, 0, 0, 0)),
                   pl.BlockSpec((None, LANES, 2 * pw), lambda i: (i, 0, 0)),
                   pl.BlockSpec((None, 2 * pw, LANES), lambda i: (i, 0, 0)),
                   pl.BlockSpec((None, 2, 2 * pw), lambda i: (i, 0, 0))],
        out_shape=[jax.ShapeDtypeStruct((nl, chunk, LANES, LANES), BF16),
                   jax.ShapeDtypeStruct((nl, chunk, LANES, 2 * pw), BF16),
                   jax.ShapeDtypeStruct((nl, chunk, 2 * pw, LANES), BF16),
                   jax.ShapeDtypeStruct((nl, LANES, 2 * pw), F32),
                   jax.ShapeDtypeStruct((nl, 2 * pw, LANES), F32),
                   jax.ShapeDtypeStruct((nl, 2, 2 * pw), F32)],
        compiler_params=_params("parallel"),
    )(*args)


def _s5_prompt_kernel(u_ref, d_ref, kt_ref, wt_ref, et_ref, ap_ref, z_ref, hre_ref, him_ref, s_sc,
                      *, chunk, bsz):
    m = u_ref.shape[0]
    rows = m // chunk
    nchunk = rows // bsz
    pw = ap_ref.shape[1] // 2
    u = [u_ref[pl.ds(i, rows, stride=chunk), :] for i in range(chunk)]
    ub = [x.astype(BF16) for x in u]
    s = _dot(ub[0], wt_ref[chunk - 1])
    for i in range(1, chunk):
        s = s + _dot(ub[i], wt_ref[chunk - 1 - i])
    s_sc[...] = s
    a_r = ap_ref[1:2, 0:pw]
    a_i = ap_ref[1:2, pw:2 * pw]
    def body(g8, carry):
        out = []
        for b in range(bsz):
            hr, hi = carry[2 * b], carry[2 * b + 1]
            base = pl.multiple_of(b * nchunk + g8 * SUBLANES, SUBLANES)
            tile = s_sc[pl.ds(base, SUBLANES), :]
            before_r, before_i = [], []
            for r in range(SUBLANES):
                before_r.append(hr)
                before_i.append(hi)
                nr = a_r * hr - a_i * hi + tile[r:r + 1, 0:pw]
                ni = a_r * hi + a_i * hr + tile[r:r + 1, pw:2 * pw]
                hr, hi = nr, ni
            s_sc[pl.ds(base, SUBLANES), 0:pw] = jnp.concatenate(before_r, axis=0)
            s_sc[pl.ds(base, SUBLANES), pw:2 * pw] = jnp.concatenate(before_i, axis=0)
            out += [hr, hi]
        return tuple(out)

    zero = jnp.zeros((1, pw), F32)
    last = lax.fori_loop(0, nchunk // SUBLANES, body, (zero,) * (2 * bsz))
    for b in range(bsz):
        hre_ref[b:b + 1, :] = last[2 * b]
        him_ref[b:b + 1, :] = last[2 * b + 1]
    h_in = s_sc[...].astype(BF16)
    d = d_ref[...]
    for i in range(chunk):
        y = _dot(h_in, et_ref[i])
        for j in range(i + 1):
            y = y + _dot(ub[j], kt_ref[i - j])
        y = y + d * u[i]
        z_ref[pl.ds(i, rows, stride=chunk), :] = _gelu_tanh(y)


def s5_prompt(u, d_skip, kt, wt, et, ap, bsz):
    m, width = u.shape
    nl = width // LANES
    chunk = kt.shape[1]
    pw2 = ap.shape[2]
    rows = m // chunk
    assert (m // bsz) % (chunk * SUBLANES) == 0
    col = pl.BlockSpec((m, LANES), lambda i: (0, i))
    tab = lambda a: pl.BlockSpec((None,) + a.shape[1:], lambda i: (i,) + (0,) * (a.ndim - 1))
    st = pl.BlockSpec((bsz, pw2 // 2), lambda i: (0, i))
    return pl.pallas_call(
        functools.partial(_s5_prompt_kernel, chunk=chunk, bsz=bsz),
        grid=(nl,),
        in_specs=[col, pl.BlockSpec((1, LANES), lambda i: (0, i)), tab(kt), tab(wt), tab(et), tab(ap)],
        out_specs=[col, st, st],
        out_shape=[jax.ShapeDtypeStruct((m, width), F32),
                   jax.ShapeDtypeStruct((bsz, nl * pw2 // 2), F32),
                   jax.ShapeDtypeStruct((bsz, nl * pw2 // 2), F32)],
        scratch_shapes=[pltpu.VMEM((rows, pw2), F32)],
        compiler_params=_params("parallel"),
    )(u, d_skip.reshape(1, width), kt, wt, et, ap)


def _s5_sample_kernel(u_ref, d_ref, hre_ref, him_ref, w0_ref, e0_ref, ap_ref, z_ref, ore_ref, oim_ref):
    pw = hre_ref.shape[1]
    u = u_ref[...]
    bu = _dot_f32(u, w0_ref[...])
    a_r = ap_ref[0:1, 0:pw]
    a_i = ap_ref[0:1, pw:2 * pw]
    hr, hi = hre_ref[...], him_ref[...]
    nr = a_r * hr - a_i * hi + bu[:, 0:pw]
    ni = a_r * hi + a_i * hr + bu[:, pw:2 * pw]
    y = _dot_f32(jnp.concatenate([nr, ni], axis=1), e0_ref[...]) + d_ref[...] * u
    z_ref[...] = _gelu_tanh(y)
    ore_ref[...] = nr
    oim_ref[...] = ni


def s5_sample(u, d_skip, h_re, h_im, w0, e0, ap):
    m, width = u.shape
    nl = width // LANES
    pw2 = ap.shape[2]
    col = pl.BlockSpec((m, LANES), lambda i: (0, i))
    tab = lambda a: pl.BlockSpec((None,) + a.shape[1:], lambda i: (i,) + (0,) * (a.ndim - 1))
    st = pl.BlockSpec((m, pw2 // 2), lambda i: (0, i))
    return pl.pallas_call(
        _s5_sample_kernel,
        grid=(nl,),
        in_specs=[col, pl.BlockSpec((1, LANES), lambda i: (0, i)), st, st, tab(w0), tab(e0), tab(ap)],
        out_specs=[col, st, st],
        out_shape=[jax.ShapeDtypeStruct((m, width), F32),
                   jax.ShapeDtypeStruct(h_re.shape, F32),
                   jax.ShapeDtypeStruct(h_im.shape, F32)],
        compiler_params=_params("parallel"),
    )(u, d_skip.reshape(1, width), h_re, h_im, w0, e0, ap)


def _layer(xp, xs, l, big_w, lw, kv, dims, prompt, st):
    cw, fw, sw, heads, dh, d_ff = dims
    m, d = xp.shape
    ms = xs.shape[0]
    depth = big_w["w_in_t"].shape[0]
    tm = min(1024, m)
    xn, xn_s = rmsnorm(xp, lw["norm_mix"], BF16), rmsnorm(xs, lw["norm_mix"], BF16)
    proj = lambda off, n, ep, dt, extras=(), stack=None, tn=min(1024, cw): matmul_ws(
        xn, xn_s, [(big_w["w_in_t"], off)], n, ep, list(extras), dt, tm, tn, l, transposed=True,
        stack=stack, w_single_buffer=True)
    f0 = 3 * cw + 3 * fw
    hbc, hbc_s = proj(0, 3 * cw, _ep_plain, F32)
    q, q_s = proj(3 * cw, fw, _ep_plain, BF16)
    k_all = proj(3 * cw + fw, fw, _ep_plain, F32, stack=(kv and kv[0], l, depth))
    v_all = proj(3 * cw + 2 * fw, fw, _ep_plain, F32, stack=(kv and kv[1], l, depth))
    logf, logf_s = proj(f0, LANES, _ep_logf, F32, extras=[(lw["fox_bias"], None, "row", 0)], tn=LANES)
    u_c, u_c_s = proj(f0 + heads, sw, _ep_plain, F32)
    sg, sg_s = proj(f0 + heads + sw, 3 * d, _ep_sigmoid, F32)

    kt, wt, et, w0, e0, ap = lw["s5_tables"]
    bsz, t = prompt
    y_a, conv_new = conv_prompt(hbc, lw["conv_wt"], bsz, t)
    f, ft = forget_cumsum(logf, bsz, t)
    y_b = fox_prompt(q, k_all[0], v_all[0], l, f, ft, bsz, t, heads, dh)
    z, h_re, h_im = s5_prompt(u_c, lw["ssm_d"], kt, wt, et, ap, bsz)

    buf = st["conv"]
    y_a_s, u_a = conv_sample(hbc_s, buf[:, 0], buf[:, 1], lw["conv_wt"])
    conv_new_s = jnp.stack([buf[:, 1], u_a], axis=1)
    ps = st["cache_k"].shape[2]
    after = forget_after(st["page_table"], st["cache_logf"], l)
    db, npg = st["page_table"].shape
    y_b_s = fox_sample(st["page_table"], q_s.reshape(ms, heads, dh), k_all[1][l].reshape(ms, heads, dh),
                       v_all[1][l].reshape(ms, heads, dh),
                       jnp.tile(logf_s[:, :heads], (1, ps)).reshape(db, 1, ps * heads),
                       after.reshape(db, npg, 1, ps * heads), st["cache_k"], st["cache_v"], l)
    y_b_s = y_b_s.reshape(ms, fw)
    z_s, h_re_s, h_im_s = s5_sample(u_c_s, lw["ssm_d"], st["h_re"], st["h_im"], w0, e0, ap)

    glu = lambda zz: matmul(zz, [(big_w["w_glu_bf"], 0)], sw, _ep_glu, [(zz, "tile", 0)], BF16, tm, 1024, layer=l)
    merged, merged_s = gated_merge((y_a, y_b, glu(z)), (y_a_s, y_b_s, glu(z_s)), big_w["w_branch"], l,
                                   sg, sg_s, tm, 512)
    xp, xs = matmul_ws(merged, merged_s, [(big_w["w_out"], 0)], d, _ep_residual, [(xp, xs, "tile", 0)],
                       F32, tm, 512, l)
    xn2, xn2_s = rmsnorm(xp, lw["norm_ffn"], BF16), rmsnorm(xs, lw["norm_ffn"], BF16)
    tf = 256
    hid, hid_s = matmul_ws(xn2, xn2_s, [(big_w["w_up"], 0), (big_w["w_up"], d_ff // tf)], d_ff, _ep_swiglu,
                           [], BF16, 2 * tm, tf, l)
    down = lambda hh, xx: matmul(hh, [(big_w["w_down_bf"], 0)], d, _ep_residual, [(xx, "tile", 0)], F32,
                                 512, 512, layer=l)
    xp, xs = down(hid, xp), down(hid_s, xs)
    return (xp, xs, (k_all, v_all), (logf, conv_new, h_re, h_im), (logf_s, conv_new_s, h_re_s, h_im_s))


def kernel(x_prompt, x_sample, cache_k, cache_v, cache_logf, state_conv, state_ssm_re, state_ssm_im,
           page_table, norm_mix, w_in, conv_w, fox_bias, ssm_a_re, ssm_a_im, ssm_log_dt, ssm_b_re,
           ssm_b_im, ssm_c_re, ssm_c_im, ssm_d, w_glu, w_branch, w_out, norm_ffn, w_up, w_down,
           norm_final):
    bp, t, d = x_prompt.shape
    bd, td, _ = x_sample.shape
    depth = w_in.shape[0]
    cw = conv_w.shape[1]
    heads = fox_bias.shape[1]
    dh = cache_k.shape[4]
    fw = heads * dh
    g, p = ssm_a_re.shape[1:]
    sw = ssm_d.shape[1]
    d_ff = w_down.shape[1]
    assert td == 1
    dims = (cw, fw, sw, heads, dh, d_ff)

    big_w = {"w_in_t": jnp.swapaxes(w_in, 1, 2), "w_branch": w_branch, "w_out": w_out, "w_up": w_up,
             "w_glu_bf": w_glu.astype(BF16), "w_down_bf": w_down.astype(BF16)}
    xp = x_prompt.reshape(bp * t, d)
    xs = x_sample.reshape(bd * td, d)
    kv = None
    outs_p, outs_s = [], []
    for l in range(depth):
        lw = {
            "norm_mix": norm_mix[l], "norm_ffn": norm_ffn[l],
            "fox_bias": jnp.pad(fox_bias[l], (0, LANES - heads)).reshape(1, LANES),
            "conv_wt": conv_w[l].T,
            "ssm_d": ssm_d[l],
            "s5_tables": s5_tables(ssm_a_re[l], ssm_a_im[l], ssm_log_dt[l], ssm_b_re[l], ssm_b_im[l],
                                   ssm_c_re[l], ssm_c_im[l]),
        }
        st = {"conv": state_conv[l], "h_re": state_ssm_re[l].reshape(bd, g * p),
              "h_im": state_ssm_im[l].reshape(bd, g * p), "page_table": page_table,
              "cache_k": cache_k, "cache_v": cache_v, "cache_logf": cache_logf}
        xp, xs, kv, (lfp, cvp, hrp, hip), (lfs, cvs, hrs, his) = _layer(xp, xs, l, big_w, lw, kv, dims, (bp, t), st)
        outs_p.append((lfp[:, :heads].reshape(bp, t, heads), cvp, hrp.reshape(bp, g, p), hip.reshape(bp, g, p)))
        outs_s.append((lfs[:, :heads].reshape(bd, td, heads), cvs, hrs.reshape(bd, g, p), his.reshape(bd, g, p)))
    stack = lambda outs: [jnp.stack(a, axis=0) for a in zip(*outs)]
    y_prompt = rmsnorm(xp, norm_final, F32).reshape(bp, t, d)
    y_sample = rmsnorm(xs, norm_final, F32).reshape(bd, td, d)
    kv5 = lambda a, b, n: a.reshape(depth, b, n, heads, dh)
    (k_p, k_s), (v_p, v_s) = kv
    return (y_prompt, y_sample, kv5(k_p, bp, t), kv5(v_p, bp, t), *stack(outs_p),
            kv5(k_s, bd, td), kv5(v_s, bd, td), *stack(outs_s))
```
